```python
import math
import jax, jax.numpy as jnp
from jax import lax
import numpy as np

D_MODEL = 2048
BATCH = 1
SEQ = 8192
DEPTH = 1
DEC_BATCH = 4
DEC_SEQ = 8192
PAST_LEN = 128

GRID_W = 64
NA_HEADS = 8
NA_HEAD_DIM = D_MODEL // 16
NA_WIDTH = NA_HEADS * NA_HEAD_DIM
NA_WIN_ROWS = 8
NA_WIN_COLS = 16
RET_HEADS = 8
RET_QK_DIM = D_MODEL // 16
RET_V_DIM = 2 * RET_QK_DIM
RET_QK_WIDTH = RET_HEADS * RET_QK_DIM
RET_V_WIDTH = RET_HEADS * RET_V_DIM
RET_CHUNK = 128
ROPE_BASE = 10000.0
D_FF = 4 * D_MODEL
IN_COLS = 3 * NA_WIDTH + 2 * RET_QK_WIDTH + 2 * RET_V_WIDTH + 2 * D_MODEL
EPS = 1e-6

kernel_name = "hybrid_natten_retention_gated_encoder"


def rms_norm(x, w):
    x32 = x.astype(jnp.float32)
    y = x32 * lax.rsqrt(jnp.mean(x32 * x32, axis=-1, keepdims=True) + EPS)
    return (y * w.astype(jnp.float32)).astype(x.dtype)


def rotary(x):
    L, d = x.shape[1], x.shape[-1]
    half = d // 2
    inv_freq = ROPE_BASE ** (-jnp.arange(half, dtype=jnp.float32) / half)
    ang = jnp.arange(L, dtype=jnp.float32)[:, None] * inv_freq[None, :]
    cos = jnp.cos(ang)[None, :, None, :]
    sin = jnp.sin(ang)[None, :, None, :]
    x1, x2 = x[..., :half], x[..., half:]
    return jnp.concatenate([x1 * cos - x2 * sin, x1 * sin + x2 * cos], axis=-1)


def neighbourhood_attention(q, k, v, rpb):
    B, L, _ = q.shape
    rows = L // GRID_W
    wr = min(NA_WIN_ROWS, rows)
    shp = (B, rows, GRID_W, NA_HEADS, NA_HEAD_DIM)
    q, k, v = q.reshape(shp), k.reshape(shp), v.reshape(shp)
    c = np.arange(GRID_W)
    c0 = np.clip(c - NA_WIN_COLS // 2, 0, GRID_W - NA_WIN_COLS)
    col_idx = c0[:, None] + np.arange(NA_WIN_COLS)[None, :]
    col_off = col_idx - c[:, None] + (NA_WIN_COLS - 1)
    rpb_cols = rpb[:, :, col_off].astype(jnp.float32)
    scale = NA_HEAD_DIM ** -0.5

    def row_block(r):
        r0 = jnp.clip(r - wr // 2, 0, rows - wr)
        q_r = lax.dynamic_index_in_dim(q, r, axis=1, keepdims=False)
        kb = lax.dynamic_slice_in_dim(k, r0, wr, axis=1)
        vb = lax.dynamic_slice_in_dim(v, r0, wr, axis=1)
        kw = kb[:, :, col_idx]
        vw = vb[:, :, col_idx]
        row_off = r0 + jnp.arange(wr) - r + (NA_WIN_ROWS - 1)
        bias = jnp.transpose(rpb_cols[:, row_off], (0, 2, 1, 3))
        s = jnp.einsum('bchd,brcwhd->bhcrw', q_r, kw).astype(jnp.float32) * scale + bias
        p = jax.nn.softmax(s.reshape(B, NA_HEADS, GRID_W, wr * NA_WIN_COLS), axis=-1)
        p = p.reshape(B, NA_HEADS, GRID_W, wr, NA_WIN_COLS).astype(v.dtype)
        return jnp.einsum('bhcrw,brcwhd->bchd', p, vw)

    out = lax.map(row_block, jnp.arange(rows))
    return jnp.moveaxis(out, 0, 1).reshape(B, L, NA_WIDTH)


def retention_direction(q, k, v, log_gamma, backward):
    _, b, h, c, dk = q.shape
    dv = v.shape[-1]
    pos = jnp.arange(c, dtype=jnp.float32)
    diff = pos[:, None] - pos[None, :]
    lg = log_gamma[:, None, None]
    if backward:
        decay_in = jnp.exp(jnp.where(diff < 0, -diff * lg, -jnp.inf))
        q_dec = jnp.exp((c - pos)[None, :] * log_gamma[:, None])
        k_dec = jnp.exp(pos[None, :] * log_gamma[:, None])
    else:
        decay_in = jnp.exp(jnp.where(diff >= 0, diff * lg, -jnp.inf))
        q_dec = jnp.exp((pos + 1.0)[None, :] * log_gamma[:, None])
        k_dec = jnp.exp((c - 1.0 - pos)[None, :] * log_gamma[:, None])
    chunk_dec = jnp.exp(c * log_gamma)[:, None, None]

    def step(state, inputs):
        qc, kc, vc = inputs
        scores = jnp.einsum('bhjd,bhld->bhjl', qc, kc) * decay_in
        out = (jnp.einsum('bhjl,bhle->bhje', scores, vc)
               + jnp.einsum('bhjd,bhde->bhje', qc * q_dec[..., None], state))
        state = chunk_dec * state + jnp.einsum('bhld,bhle->bhde', kc * k_dec[..., None], vc)
        return state, out

    state0 = jnp.zeros((b, h, dk, dv), jnp.float32)
    _, out = lax.scan(step, state0, (q, k, v), reverse=backward)
    return out


def bidirectional_retention(q, k, v, g, decay_fwd, decay_bwd):
    B, L, _ = q.shape
    n = L // RET_CHUNK
    q = rotary(q.astype(jnp.float32).reshape(B, L, RET_HEADS, RET_QK_DIM))
    k = rotary(k.astype(jnp.float32).reshape(B, L, RET_HEADS, RET_QK_DIM)) * (RET_QK_DIM ** -0.5)
    v = v.astype(jnp.float32).reshape(B, L, RET_HEADS, RET_V_DIM)

    def to_chunks(t):
        t = jnp.transpose(t, (0, 2, 1, 3)).reshape(B, RET_HEADS, n, RET_CHUNK, t.shape[-1])
        return jnp.moveaxis(t, 2, 0)

    qc, kc, vc = to_chunks(q), to_chunks(k), to_chunks(v)
    lg_f = jax.nn.log_sigmoid(decay_fwd.astype(jnp.float32))
    lg_b = jax.nn.log_sigmoid(decay_bwd.astype(jnp.float32))
    o = retention_direction(qc, kc, vc, lg_f, False) + retention_direction(qc, kc, vc, lg_b, True)
    o = jnp.moveaxis(o, 0, 2).reshape(B, RET_HEADS, L, RET_V_DIM)
    o = jnp.transpose(o, (0, 2, 1, 3))
    o = o * lax.rsqrt(jnp.mean(o * o, axis=-1, keepdims=True) + EPS)
    o = o * jax.nn.silu(g.astype(jnp.float32).reshape(B, L, RET_HEADS, RET_V_DIM))
    return o.reshape(B, L, RET_V_WIDTH)


def token_mixer(xn, w_in, rpb, decay_fwd, decay_bwd, w_na_out, w_ret_out, w_out):
    z = xn @ w_in
    splits = list(np.cumsum([NA_WIDTH, NA_WIDTH, NA_WIDTH, RET_QK_WIDTH, RET_QK_WIDTH,
                             RET_V_WIDTH, RET_V_WIDTH, D_MODEL]))
    na_q, na_k, na_v, r_q, r_k, r_v, r_g, gate_a, gate_b = jnp.split(z, splits, axis=-1)
    a = neighbourhood_attention(na_q, na_k, na_v, rpb) @ w_na_out
    r = bidirectional_retention(r_q, r_k, r_v, r_g, decay_fwd, decay_bwd).astype(xn.dtype) @ w_ret_out
    m = jax.nn.sigmoid(gate_a) * a + jax.nn.sigmoid(gate_b) * r
    return m @ w_out


def squared_relu_mlp(xn, w_up, w_down):
    hdn = jax.nn.relu(xn @ w_up)
    return (hdn * hdn) @ w_down


def encoder(x, norm_mix_w, w_in, na_rpb, ret_decay_fwd, ret_decay_bwd, w_na_out, w_ret_out,
            w_out, norm_mlp_w, w_mlp_up, w_mlp_down, norm_final_w):
    for l in range(DEPTH):
        h = x + token_mixer(rms_norm(x, norm_mix_w[l]), w_in[l], na_rpb[l], ret_decay_fwd[l],
                            ret_decay_bwd[l], w_na_out[l], w_ret_out[l], w_out[l])
        x = h + squared_relu_mlp(rms_norm(h, norm_mlp_w[l]), w_mlp_up[l], w_mlp_down[l])
    return rms_norm(x, norm_final_w)


def setup_inputs(seed: int = 0) -> dict:
    key = jax.random.key(seed)
    ks = jax.random.split(key, 16)
    f32 = jnp.float32
    nrm = lambda k, shape, s: jax.random.normal(k, shape, f32) * s
    gamma0 = 1.0 - 2.0 ** (-5.0 - np.arange(RET_HEADS))
    logit0 = jnp.asarray(np.log(gamma0) - np.log1p(-gamma0), dtype=f32)
    return {
        "x_prompt": nrm(ks[0], (BATCH, SEQ, D_MODEL), 1.0),
        "x_sample": nrm(ks[1], (DEC_BATCH, DEC_SEQ, D_MODEL), 1.0),
        "norm_mix_w": 1.0 + nrm(ks[2], (DEPTH, D_MODEL), 0.01),
        "w_in": nrm(ks[3], (DEPTH, D_MODEL, IN_COLS), D_MODEL ** -0.5),
        "na_rpb": nrm(ks[4], (DEPTH, NA_HEADS, 2 * NA_WIN_ROWS - 1, 2 * NA_WIN_COLS - 1), 0.02),
        "ret_decay_fwd": logit0[None, :] + nrm(ks[5], (DEPTH, RET_HEADS), 0.01),
        "ret_decay_bwd": logit0[None, :] + nrm(ks[6], (DEPTH, RET_HEADS), 0.01),
        "w_na_out": nrm(ks[7], (DEPTH, NA_WIDTH, D_MODEL), NA_WIDTH ** -0.5),
        "w_ret_out": nrm(ks[8], (DEPTH, RET_V_WIDTH, D_MODEL), RET_V_WIDTH ** -0.5),
        "w_out": nrm(ks[9], (DEPTH, D_MODEL, D_MODEL), D_MODEL ** -0.5),
        "norm_mlp_w": 1.0 + nrm(ks[10], (DEPTH, D_MODEL), 0.01),
        "w_mlp_up": nrm(ks[11], (DEPTH, D_MODEL, D_FF), D_MODEL ** -0.5),
        "w_mlp_down": nrm(ks[12], (DEPTH, D_FF, D_MODEL), D_FF ** -0.5),
        "norm_final_w": 1.0 + nrm(ks[13], (D_MODEL,), 0.01),
    }


def reference(x_prompt, x_sample, norm_mix_w, w_in, na_rpb, ret_decay_fwd, ret_decay_bwd,
              w_na_out, w_ret_out, w_out, norm_mlp_w, w_mlp_up, w_mlp_down, norm_final_w):
    y_prompt = encoder(x_prompt, norm_mix_w, w_in, na_rpb, ret_decay_fwd, ret_decay_bwd, w_na_out,
                       w_ret_out, w_out, norm_mlp_w, w_mlp_up, w_mlp_down, norm_final_w)
    y_sample = encoder(x_sample, norm_mix_w, w_in, na_rpb, ret_decay_fwd, ret_decay_bwd, w_na_out,
                       w_ret_out, w_out, norm_mlp_w, w_mlp_up, w_mlp_down, norm_final_w)
    return (y_prompt, y_sample)
```

```python
import functools

import jax
import jax.numpy as jnp
import numpy as np
from jax import lax
from jax.experimental import pallas as pl
from jax.experimental.pallas import tpu as pltpu

F32 = jnp.float32
BF16 = jnp.bfloat16

D_MODEL = 2048
GRID_W = 64
NA_HEADS = 8
NA_HEAD_DIM = 128
NA_WIDTH = NA_HEADS * NA_HEAD_DIM
NA_WIN_ROWS = 8
NA_WIN_COLS = 16
RET_HEADS = 8
RET_QK_DIM = 128
RET_V_DIM = 256
RET_QK_WIDTH = RET_HEADS * RET_QK_DIM
RET_V_WIDTH = RET_HEADS * RET_V_DIM
RET_CHUNK = 128
ROPE_BASE = 10000.0
D_FF = 4 * D_MODEL
EPS = 1e-6

OFF_NA_Q = 0
OFF_NA_K = OFF_NA_Q + NA_WIDTH
OFF_NA_V = OFF_NA_K + NA_WIDTH
OFF_RET_Q = OFF_NA_V + NA_WIDTH
OFF_RET_K = OFF_RET_Q + RET_QK_WIDTH
OFF_RET_V = OFF_RET_K + RET_QK_WIDTH
OFF_RET_G = OFF_RET_V + RET_V_WIDTH
OFF_GATE_A = OFF_RET_G + RET_V_WIDTH
OFF_GATE_B = OFF_GATE_A + D_MODEL
IN_COLS = OFF_GATE_B + D_MODEL

MASK_VALUE = -1e30
V7X_VMEM_BYTES = 64 * 1024 * 1024


def _compiler_params(semantics, block_bytes):
    limit = min(2 * block_bytes + (8 << 20), V7X_VMEM_BYTES - (4 << 20))
    return pltpu.CompilerParams(dimension_semantics=semantics, vmem_limit_bytes=int(limit))


IN_TM = 1024
IN_TN = 1024


def _inproj_kernel(x_ref, nw_ref, w_ref, z_ref, xn_ref):
    @pl.when(pl.program_id(1) == 0)
    def _():
        x = x_ref[...]
        ms = jnp.mean(x * x, axis=-1, keepdims=True)
        xn_ref[...] = (x * lax.rsqrt(ms + EPS) * nw_ref[...]).astype(BF16)

    z_ref[...] = jnp.dot(xn_ref[...], w_ref[...], preferred_element_type=F32).astype(z_ref.dtype)


def _inproj(x2d, norm_w, w_in):
    m = x2d.shape[0]
    tm, tn = IN_TM, IN_TN
    blocks = tm * D_MODEL * 4 + D_MODEL * tn * 2 + tm * tn * 2 + tm * D_MODEL * 2
    return pl.pallas_call(
        _inproj_kernel,
        out_shape=jax.ShapeDtypeStruct((m, IN_COLS), BF16),
        grid=(m // tm, IN_COLS // tn),
        in_specs=[
            pl.BlockSpec((tm, D_MODEL), lambda i, j: (i, 0)),
            pl.BlockSpec((1, D_MODEL), lambda i, j: (0, 0)),
            pl.BlockSpec((D_MODEL, tn), lambda i, j: (0, j)),
        ],
        out_specs=pl.BlockSpec((tm, tn), lambda i, j: (i, j)),
        scratch_shapes=[pltpu.VMEM((tm, D_MODEL), BF16)],
        compiler_params=_compiler_params(("parallel", "arbitrary"), blocks),
        name="norm_inproj",
    )(x2d, norm_w.reshape(1, D_MODEL), w_in)


NA_Q_ROWS = 8
NA_K_ROWS = 16
NA_TQ = NA_Q_ROWS * GRID_W
NA_TK = NA_K_ROWS * GRID_W


def _na_bias_table(rpb, rows):
    c = np.arange(GRID_W)
    c0 = np.clip(c - NA_WIN_COLS // 2, 0, GRID_W - NA_WIN_COLS)
    kc = np.arange(GRID_W)
    col_valid = (kc[None, :] >= c0[:, None]) & (kc[None, :] < c0[:, None] + NA_WIN_COLS)
    col_idx = np.clip(kc[None, :] - c[:, None] + NA_WIN_COLS - 1, 0, 2 * NA_WIN_COLS - 2)
    tc = jnp.where(col_valid[None, None], rpb.astype(F32)[:, :, col_idx], MASK_VALUE)
    invalid = 2 * NA_WIN_ROWS - 1
    tc = jnp.concatenate([tc, jnp.full((NA_HEADS, 1, GRID_W, GRID_W), MASK_VALUE, F32)], axis=1)

    ri = np.arange(NA_Q_ROWS)
    kri = np.arange(NA_K_ROWS)
    half = NA_WIN_ROWS // 2
    n_steps = rows // NA_Q_ROWS
    row_idx = []
    for kind in range(3):
        q_base = {0: 0, 1: NA_Q_ROWS, 2: (n_steps - 1) * NA_Q_ROWS}[kind]
        k_base = int(np.clip(q_base - half, 0, rows - NA_K_ROWS))
        r = q_base + ri
        r0 = np.clip(r - half, 0, rows - NA_WIN_ROWS)
        kr = k_base + kri
        valid = (kr[None, :] >= r0[:, None]) & (kr[None, :] < r0[:, None] + NA_WIN_ROWS)
        idx = kr[None, :] - r[:, None] + NA_WIN_ROWS - 1
        row_idx.append(np.where(valid, idx, invalid))
    row_idx = np.stack(row_idx)
    bias = tc[:, row_idx]
    bias = jnp.transpose(bias, (1, 0, 2, 4, 3, 5))
    return bias.reshape(3, NA_HEADS, NA_TQ, NA_TK)


def _na_kernel(q_ref, k_ref, v_ref, bias_ref, o_ref, *, seq):
    step = pl.program_id(2)
    start = jnp.clip(step * NA_TQ - (NA_WIN_ROWS // 2) * GRID_W, 0, seq - NA_TK)
    start = pl.multiple_of(start, (NA_WIN_ROWS // 2) * GRID_W)
    q = q_ref[...]
    k = k_ref[pl.ds(start, NA_TK), :]
    v = v_ref[pl.ds(start, NA_TK), :]
    s = lax.dot_general(q, k, (((1,), (1,)), ((), ())), preferred_element_type=F32)
    s = s * (NA_HEAD_DIM ** -0.5) + bias_ref[...]
    m = jnp.max(s, axis=-1, keepdims=True)
    p = jnp.exp(s - m)
    l = jnp.sum(p, axis=-1, keepdims=True)
    o = jnp.dot(p.astype(BF16), v, preferred_element_type=F32)
    o_ref[...] = (o / l).astype(o_ref.dtype)


def _neighbourhood_attention(z3, bias):
    b, seq, _ = z3.shape
    n_steps = seq // NA_TQ
    hd = NA_HEAD_DIM

    def kind(s):
        return jnp.where(s == 0, 0, jnp.where(s == n_steps - 1, 2, 1))

    blocks = NA_TQ * hd * 2 * 2 + 2 * seq * hd * 2 + NA_TQ * NA_TK * 4 + 4 * NA_TQ * NA_TK * 4
    return pl.pallas_call(
        functools.partial(_na_kernel, seq=seq),
        out_shape=jax.ShapeDtypeStruct((b, seq, NA_WIDTH), BF16),
        grid=(NA_HEADS, b, n_steps),
        in_specs=[
            pl.BlockSpec((None, NA_TQ, hd), lambda h, bi, s: (bi, s, OFF_NA_Q // hd + h)),
            pl.BlockSpec((None, seq, hd), lambda h, bi, s: (bi, 0, OFF_NA_K // hd + h)),
            pl.BlockSpec((None, seq, hd), lambda h, bi, s: (bi, 0, OFF_NA_V // hd + h)),
            pl.BlockSpec((None, None, NA_TQ, NA_TK), lambda h, bi, s: (kind(s), h, 0, 0)),
        ],
        out_specs=pl.BlockSpec((None, NA_TQ, hd), lambda h, bi, s: (bi, s, h)),
        compiler_params=_compiler_params(("parallel", "parallel", "arbitrary"), blocks),
        name="neighbourhood_attention",
    )(z3, z3, z3, bias)


RET_TB = 1024
RET_NC = RET_TB // RET_CHUNK


def _rope_tables(seq):
    half = RET_QK_DIM // 2
    inv_freq = ROPE_BASE ** (-jnp.arange(half, dtype=F32) / half)
    ang = jnp.arange(seq, dtype=F32)[:, None] * inv_freq[None, :]
    cos, sin = jnp.cos(ang), jnp.sin(ang)
    return jnp.concatenate([cos, cos], axis=-1), jnp.concatenate([-sin, sin], axis=-1)


def _decay_tables(decay, backward):
    c = RET_CHUNK
    lg = jax.nn.log_sigmoid(decay.astype(F32))
    pos = jnp.arange(c, dtype=F32)
    diff = pos[:, None] - pos[None, :]
    lg3 = lg[:, None, None]
    if backward:
        decay_in = jnp.exp(jnp.where(diff < 0, -diff * lg3, -jnp.inf))
        q_dec = jnp.exp((c - pos)[None, :] * lg[:, None])
        k_dec = jnp.exp(pos[None, :] * lg[:, None])
    else:
        decay_in = jnp.exp(jnp.where(diff >= 0, diff * lg3, -jnp.inf))
        q_dec = jnp.exp((pos + 1.0)[None, :] * lg[:, None])
        k_dec = jnp.exp((c - 1.0 - pos)[None, :] * lg[:, None])
    chunk_dec = jnp.exp(c * lg)
    q_dec = jnp.broadcast_to(q_dec[:, :, None], (RET_HEADS, c, RET_QK_DIM))
    k_dec_t = jnp.broadcast_to(k_dec[:, None, :], (RET_HEADS, RET_QK_DIM, c))
    return decay_in, q_dec, k_dec_t, chunk_dec


def _retention_scan(cd_ref, q_ref, k_ref, v_ref, cos_ref, sin_ref, din_ref, qd_ref, kdt_ref, state_ref,
                    emit, *, backward):
    @pl.when(pl.program_id(2) == 0)
    def _():
        state_ref[...] = jnp.zeros_like(state_ref)

    cd = cd_ref[pl.program_id(1)]
    cos, sin = cos_ref[...], sin_ref[...]
    q = q_ref[...].astype(F32)
    k = k_ref[...].astype(F32)
    half = RET_QK_DIM // 2
    q = q * cos + pltpu.roll(q, half, 1) * sin
    k = (k * cos + pltpu.roll(k, half, 1) * sin) * (RET_QK_DIM ** -0.5)
    kt = k.T
    din, qd, kdt = din_ref[...], qd_ref[...], kdt_ref[...]
    state = state_ref[...]
    order = range(RET_NC - 1, -1, -1) if backward else range(RET_NC)
    for c in order:
        sl = slice(c * RET_CHUNK, (c + 1) * RET_CHUNK)
        qc = q[sl]
        ktc = kt[:, sl]
        vc = v_ref[sl, :]
        scores = jnp.dot(qc.astype(BF16), ktc.astype(BF16), preferred_element_type=F32) * din
        out = (jnp.dot(scores.astype(BF16), vc, preferred_element_type=F32)
               + jnp.dot((qc * qd).astype(BF16), state.astype(BF16), preferred_element_type=F32))
        state = cd * state + jnp.dot((ktc * kdt).astype(BF16), vc, preferred_element_type=F32)
        emit(sl, out)
    state_ref[...] = state


def _ret_fwd_kernel(cd_ref, q_ref, k_ref, v_ref, cos_ref, sin_ref, din_ref, qd_ref, kdt_ref, o_ref, state_ref):
    def emit(sl, out):
        o_ref[sl, :] = out

    _retention_scan(cd_ref, q_ref, k_ref, v_ref, cos_ref, sin_ref, din_ref, qd_ref, kdt_ref, state_ref,
                    emit, backward=False)


def _ret_bwd_kernel(cd_ref, q_ref, k_ref, v_ref, cos_ref, sin_ref, din_ref, qd_ref, kdt_ref, of_ref, g_ref,
                    o_ref, state_ref):
    def emit(sl, out):
        o = of_ref[sl, :] + out
        o = o * lax.rsqrt(jnp.mean(o * o, axis=-1, keepdims=True) + EPS)
        g = g_ref[sl, :].astype(F32)
        o_ref[sl, :] = (o * (g * jax.nn.sigmoid(g))).astype(o_ref.dtype)

    _retention_scan(cd_ref, q_ref, k_ref, v_ref, cos_ref, sin_ref, din_ref, qd_ref, kdt_ref, state_ref,
                    emit, backward=True)


def _retention(z3, rope, tables_f, tables_b):
    b, seq, _ = z3.shape
    n_blk = seq // RET_TB
    dk, dv, c = RET_QK_DIM, RET_V_DIM, RET_CHUNK
    cos2, sin2 = rope

    def specs(tok):
        return [
            pl.BlockSpec(memory_space=pltpu.SMEM),
            pl.BlockSpec((None, RET_TB, dk), lambda bi, h, j: (bi, tok(j), OFF_RET_Q // dk + h)),
            pl.BlockSpec((None, RET_TB, dk), lambda bi, h, j: (bi, tok(j), OFF_RET_K // dk + h)),
            pl.BlockSpec((None, RET_TB, dv), lambda bi, h, j: (bi, tok(j), OFF_RET_V // dv + h)),
            pl.BlockSpec((RET_TB, dk), lambda bi, h, j: (tok(j), 0)),
            pl.BlockSpec((RET_TB, dk), lambda bi, h, j: (tok(j), 0)),
            pl.BlockSpec((None, c, c), lambda bi, h, j: (h, 0, 0)),
            pl.BlockSpec((None, c, dk), lambda bi, h, j: (h, 0, 0)),
            pl.BlockSpec((None, dk, c), lambda bi, h, j: (h, 0, 0)),
        ]

    blocks = RET_TB * (2 * dk * 2 + dv * 2 + 2 * dk * 4 + 2 * dv * 4 + dv * 2) + 3 * c * c * 4 + dk * dv * 4
    blocks += 6 * RET_TB * dk * 4
    params = _compiler_params(("parallel", "parallel", "arbitrary"), blocks)
    scratch = [pltpu.VMEM((dk, dv), F32)]

    din, qd, kdt, cd = tables_f
    fwd = lambda j: j
    o_fwd = pl.pallas_call(
        _ret_fwd_kernel,
        out_shape=jax.ShapeDtypeStruct((b, seq, RET_V_WIDTH), F32),
        grid=(b, RET_HEADS, n_blk),
        in_specs=specs(fwd),
        out_specs=pl.BlockSpec((None, RET_TB, dv), lambda bi, h, j: (bi, j, h)),
        scratch_shapes=scratch,
        compiler_params=params,
        name="retention_fwd",
    )(cd, z3, z3, z3, cos2, sin2, din, qd, kdt)

    din, qd, kdt, cd = tables_b
    rev = lambda j: n_blk - 1 - j
    return pl.pallas_call(
        _ret_bwd_kernel,
        out_shape=jax.ShapeDtypeStruct((b, seq, RET_V_WIDTH), BF16),
        grid=(b, RET_HEADS, n_blk),
        in_specs=specs(rev) + [
            pl.BlockSpec((None, RET_TB, dv), lambda bi, h, j: (bi, rev(j), h)),
            pl.BlockSpec((None, RET_TB, dv), lambda bi, h, j: (bi, rev(j), OFF_RET_G // dv + h)),
        ],
        out_specs=pl.BlockSpec((None, RET_TB, dv), lambda bi, h, j: (bi, rev(j), h)),
        scratch_shapes=scratch,
        compiler_params=params,
        name="retention_bwd",
    )(cd, z3, z3, z3, cos2, sin2, din, qd, kdt, o_fwd, z3)


MIX_TM = 1024
MIX_TN = 1024


def _mix_kernel(a_ref, r_ref, wna_ref, wret_ref, ga_ref, gb_ref, m_ref):
    a = jnp.dot(a_ref[...], wna_ref[...], preferred_element_type=F32)
    r = jnp.dot(r_ref[...], wret_ref[...], preferred_element_type=F32)
    ga = jax.nn.sigmoid(ga_ref[...].astype(F32))
    gb = jax.nn.sigmoid(gb_ref[...].astype(F32))
    m_ref[...] = (ga * a + gb * r).astype(m_ref.dtype)


def _branch_mix(a2d, r2d, z2d, w_na_out, w_ret_out):
    m = a2d.shape[0]
    tm, tn = MIX_TM, MIX_TN
    blocks = tm * NA_WIDTH * 2 + tm * RET_V_WIDTH * 2 + (NA_WIDTH + RET_V_WIDTH) * tn * 2 + 3 * tm * tn * 2
    blocks += 4 * tm * tn * 4
    return pl.pallas_call(
        _mix_kernel,
        out_shape=jax.ShapeDtypeStruct((m, D_MODEL), BF16),
        grid=(m // tm, D_MODEL // tn),
        in_specs=[
            pl.BlockSpec((tm, NA_WIDTH), lambda i, j: (i, 0)),
            pl.BlockSpec((tm, RET_V_WIDTH), lambda i, j: (i, 0)),
            pl.BlockSpec((NA_WIDTH, tn), lambda i, j: (0, j)),
            pl.BlockSpec((RET_V_WIDTH, tn), lambda i, j: (0, j)),
            pl.BlockSpec((tm, tn), lambda i, j: (i, OFF_GATE_A // tn + j)),
            pl.BlockSpec((tm, tn), lambda i, j: (i, OFF_GATE_B // tn + j)),
        ],
        out_specs=pl.BlockSpec((tm, tn), lambda i, j: (i, j)),
        compiler_params=_compiler_params(("parallel", "arbitrary"), blocks),
        name="branch_mix",
    )(a2d, r2d, w_na_out, w_ret_out, z2d, z2d)


OUT_TM = 1024
OUT_TN = 1024


def _outproj_kernel(x_ref, m_ref, w_ref, h_ref):
    h_ref[...] = x_ref[...] + jnp.dot(m_ref[...], w_ref[...], preferred_element_type=F32)


def _outproj(x2d, m2d, w_out):
    m = x2d.shape[0]
    tm, tn = OUT_TM, OUT_TN
    blocks = 2 * tm * tn * 4 + tm * D_MODEL * 2 + D_MODEL * tn * 2 + tm * tn * 4
    return pl.pallas_call(
        _outproj_kernel,
        out_shape=jax.ShapeDtypeStruct((m, D_MODEL), F32),
        grid=(m // tm, D_MODEL // tn),
        in_specs=[
            pl.BlockSpec((tm, tn), lambda i, j: (i, j)),
            pl.BlockSpec((tm, D_MODEL), lambda i, j: (i, 0)),
            pl.BlockSpec((D_MODEL, tn), lambda i, j: (0, j)),
        ],
        out_specs=pl.BlockSpec((tm, tn), lambda i, j: (i, j)),
        compiler_params=_compiler_params(("parallel", "arbitrary"), blocks),
        name="outproj_residual",
    )(x2d, m2d, w_out)


MLP_TM = 512
MLP_TF = 512


def _mlp_kernel(h_ref, nw_ref, wup_ref, wdown_ref, fw_ref, y_ref, hn_ref, acc_ref):
    f = pl.program_id(1)

    @pl.when(f == 0)
    def _():
        h = h_ref[...]
        ms = jnp.mean(h * h, axis=-1, keepdims=True)
        hn_ref[...] = (h * lax.rsqrt(ms + EPS) * nw_ref[...]).astype(BF16)
        acc_ref[...] = jnp.zeros_like(acc_ref)

    u = jnp.maximum(jnp.dot(hn_ref[...], wup_ref[...], preferred_element_type=F32), 0.0)
    acc_ref[...] += jnp.dot((u * u).astype(BF16), wdown_ref[...], preferred_element_type=F32)

    @pl.when(f == pl.num_programs(1) - 1)
    def _():
        x = h_ref[...] + acc_ref[...]
        ms = jnp.mean(x * x, axis=-1, keepdims=True)
        y_ref[...] = x * lax.rsqrt(ms + EPS) * fw_ref[...]


def _mlp(h2d, norm_w, w_up, w_down, final_w):
    m = h2d.shape[0]
    tm, tf = MLP_TM, MLP_TF
    blocks = 2 * tm * D_MODEL * 4 + 2 * D_MODEL * tf * 2 + tm * D_MODEL * 2 + tm * D_MODEL * 4 + 2 * tm * tf * 4
    return pl.pallas_call(
        _mlp_kernel,
        out_shape=jax.ShapeDtypeStruct((m, D_MODEL), F32),
        grid=(m // tm, D_FF // tf),
        in_specs=[
            pl.BlockSpec((tm, D_MODEL), lambda i, f: (i, 0)),
            pl.BlockSpec((1, D_MODEL), lambda i, f: (0, 0)),
            pl.BlockSpec((D_MODEL, tf), lambda i, f: (0, f)),
            pl.BlockSpec((tf, D_MODEL), lambda i, f: (f, 0)),
            pl.BlockSpec((1, D_MODEL), lambda i, f: (0, 0)),
        ],
        out_specs=pl.BlockSpec((tm, D_MODEL), lambda i, f: (i, 0)),
        scratch_shapes=[pltpu.VMEM((tm, D_MODEL), BF16), pltpu.VMEM((tm, D_MODEL), F32)],
        compiler_params=_compiler_params(("parallel", "arbitrary"), blocks),
        name="mlp_final_norm",
    )(h2d, norm_w.reshape(1, D_MODEL), w_up, w_down, final_w.reshape(1, D_MODEL))


def _encode(x, p):
    b, seq, d = x.shape
    x2d = x.reshape(b * seq, d)
    z2d = _inproj(x2d, p["norm_mix_w"], p["w_in"])
    z3 = z2d.reshape(b, seq, IN_COLS)
    a = _neighbourhood_attention(z3, p["na_bias"])
    r = _retention(z3, p["rope"], p["ret_fwd"], p["ret_bwd"])
    m2d = _branch_mix(a.reshape(b * seq, NA_WIDTH), r.reshape(b * seq, RET_V_WIDTH), z2d,
                      p["w_na_out"], p["w_ret_out"])
    h2d = _outproj(x2d, m2d, p["w_out"])
    y2d = _mlp(h2d, p["norm_mlp_w"], p["w_mlp_up"], p["w_mlp_down"], p["norm_final_w"])
    return y2d.reshape(b, seq, d)


def kernel(x_prompt, x_sample, norm_mix_w, w_in, na_rpb, ret_decay_fwd, ret_decay_bwd, w_na_out, w_ret_out,
           w_out, norm_mlp_w, w_mlp_up, w_mlp_down, norm_final_w):
    depth = w_in.shape[0]
    seq = x_prompt.shape[1]
    assert x_sample.shape[1] == seq and seq % RET_TB == 0 and seq % NA_TQ == 0
    rope = _rope_tables(seq)
    layers = []
    for l in range(depth):
        layers.append({
            "norm_mix_w": norm_mix_w[l].astype(F32),
            "w_in": w_in[l].astype(BF16),
            "na_bias": _na_bias_table(na_rpb[l], seq // GRID_W),
            "rope": rope,
            "ret_fwd": _decay_tables(ret_decay_fwd[l], False),
            "ret_bwd": _decay_tables(ret_decay_bwd[l], True),
            "w_na_out": w_na_out[l].astype(BF16),
            "w_ret_out": w_ret_out[l].astype(BF16),
            "w_out": w_out[l].astype(BF16),
            "norm_mlp_w": norm_mlp_w[l].astype(F32),
            "w_mlp_up": w_mlp_up[l].astype(BF16),
            "w_mlp_down": w_mlp_down[l].astype(BF16),
        })
    assert depth == 1, "final norm is fused into the last layer's MLP kernel"
    layers[0]["norm_final_w"] = norm_final_w.astype(F32)

    def encode(x):
        return _encode(x, layers[0])

    return encode(x_prompt), encode(x_sample)
```

```python
import functools

import jax
import jax.numpy as jnp
import numpy as np
from jax import lax
from jax.experimental import pallas as pl
from jax.experimental.pallas import tpu as pltpu

F32 = jnp.float32
BF16 = jnp.bfloat16

D_MODEL = 2048
GRID_W = 64
NA_HEADS = 8
NA_HEAD_DIM = 128
NA_WIDTH = NA_HEADS * NA_HEAD_DIM
NA_WIN_ROWS = 8
NA_WIN_COLS = 16
RET_HEADS = 8
RET_QK_DIM = 128
RET_V_DIM = 256
RET_QK_WIDTH = RET_HEADS * RET_QK_DIM
RET_V_WIDTH = RET_HEADS * RET_V_DIM
ROPE_BASE = 10000.0
D_FF = 4 * D_MODEL
EPS = 1e-6

OFF_NA_Q = 0
OFF_NA_K = OFF_NA_Q + NA_WIDTH
OFF_NA_V = OFF_NA_K + NA_WIDTH
OFF_RET_Q = OFF_NA_V + NA_WIDTH
OFF_RET_K = OFF_RET_Q + RET_QK_WIDTH
OFF_RET_V = OFF_RET_K + RET_QK_WIDTH
OFF_RET_G = OFF_RET_V + RET_V_WIDTH
OFF_GATE_A = OFF_RET_G + RET_V_WIDTH
OFF_GATE_B = OFF_GATE_A + D_MODEL
IN_COLS = OFF_GATE_B + D_MODEL

MASK_VALUE = -1e30
V7X_VMEM_BYTES = 64 * 1024 * 1024


def _compiler_params(semantics, block_bytes):
    limit = min(2 * block_bytes + (8 << 20), V7X_VMEM_BYTES - (4 << 20))
    return pltpu.CompilerParams(dimension_semantics=semantics, vmem_limit_bytes=int(limit))


IN_TM = 1024
IN_TN = 1024


def _inproj_kernel(x_ref, nw_ref, w_ref, z_ref, xn_ref):
    @pl.when(pl.program_id(1) == 0)
    def _():
        x = x_ref[...]
        ms = jnp.mean(x * x, axis=-1, keepdims=True)
        xn_ref[...] = (x * lax.rsqrt(ms + EPS) * nw_ref[...]).astype(BF16)

    z_ref[...] = jnp.dot(xn_ref[...], w_ref[...], preferred_element_type=F32).astype(z_ref.dtype)


def _inproj(x2d, norm_w, w_in):
    m = x2d.shape[0]
    tm, tn = IN_TM, IN_TN
    blocks = tm * D_MODEL * 4 + D_MODEL * tn * 2 + tm * tn * 2 + tm * D_MODEL * 2
    return pl.pallas_call(
        _inproj_kernel,
        out_shape=jax.ShapeDtypeStruct((m, IN_COLS), BF16),
        grid=(m // tm, IN_COLS // tn),
        in_specs=[
            pl.BlockSpec((tm, D_MODEL), lambda i, j: (i, 0)),
            pl.BlockSpec((1, D_MODEL), lambda i, j: (0, 0)),
            pl.BlockSpec((D_MODEL, tn), lambda i, j: (0, j)),
        ],
        out_specs=pl.BlockSpec((tm, tn), lambda i, j: (i, j)),
        scratch_shapes=[pltpu.VMEM((tm, D_MODEL), BF16)],
        compiler_params=_compiler_params(("parallel", "arbitrary"), blocks),
        name="norm_inproj",
    )(x2d, norm_w.reshape(1, D_MODEL), w_in)


NA_Q_ROWS = 8
NA_K_ROWS = 16
NA_TQ = NA_Q_ROWS * GRID_W
NA_TK = NA_K_ROWS * GRID_W
NA_CQ = 16
NA_CK = 32
NA_N_CHUNKS = GRID_W // NA_CQ
NA_CHUNK_Q = NA_Q_ROWS * NA_CQ
NA_CHUNK_K = NA_K_ROWS * NA_CK
SUBLANES = 8


def _na_key_col_starts():
    starts = []
    for cq in range(NA_N_CHUNKS):
        c = np.arange(cq * NA_CQ, (cq + 1) * NA_CQ)
        c0 = np.clip(c - NA_WIN_COLS // 2, 0, GRID_W - NA_WIN_COLS)
        start = min((int(c0.min()) // SUBLANES) * SUBLANES, GRID_W - NA_CK)
        assert start <= c0.min() and c0.max() + NA_WIN_COLS <= start + NA_CK
        starts.append(start)
    return tuple(starts)


NA_KEY_COL_START = _na_key_col_starts()


def _na_bias_table(rpb, rows):
    cl = np.arange(NA_CQ)
    kcl = np.arange(NA_CK)
    c = (np.arange(NA_N_CHUNKS) * NA_CQ)[:, None, None] + cl[None, :, None]
    kc = np.asarray(NA_KEY_COL_START)[:, None, None] + kcl[None, None, :]
    c0 = np.clip(c - NA_WIN_COLS // 2, 0, GRID_W - NA_WIN_COLS)
    col_valid = (kc >= c0) & (kc < c0 + NA_WIN_COLS)
    col_idx = np.clip(kc - c + NA_WIN_COLS - 1, 0, 2 * NA_WIN_COLS - 2)
    tc = jnp.where(col_valid[None, None], rpb.astype(F32)[:, :, col_idx], MASK_VALUE)
    invalid = 2 * NA_WIN_ROWS - 1
    tc = jnp.concatenate([tc, jnp.full((NA_HEADS, 1) + tc.shape[2:], MASK_VALUE, F32)], axis=1)

    ri = np.arange(NA_Q_ROWS)
    kri = np.arange(NA_K_ROWS)
    half = NA_WIN_ROWS // 2
    n_steps = rows // NA_Q_ROWS
    row_idx = []
    for kind in range(3):
        q_base = {0: 0, 1: NA_Q_ROWS, 2: (n_steps - 1) * NA_Q_ROWS}[kind]
        k_base = int(np.clip(q_base - half, 0, rows - NA_K_ROWS))
        r = q_base + ri
        r0 = np.clip(r - half, 0, rows - NA_WIN_ROWS)
        kr = k_base + kri
        valid = (kr[None, :] >= r0[:, None]) & (kr[None, :] < r0[:, None] + NA_WIN_ROWS)
        idx = kr[None, :] - r[:, None] + NA_WIN_ROWS - 1
        row_idx.append(np.where(valid, idx, invalid))
    row_idx = np.stack(row_idx)
    bias = tc[:, row_idx]
    bias = jnp.transpose(bias, (1, 0, 4, 2, 5, 3, 6))
    return bias.reshape(3, NA_HEADS, NA_N_CHUNKS, NA_CHUNK_Q, NA_CHUNK_K)


def _na_kernel(q_ref, k_ref, v_ref, bias_ref, o_ref, *, seq):
    step = pl.program_id(2)
    start = jnp.clip(step * NA_TQ - (NA_WIN_ROWS // 2) * GRID_W, 0, seq - NA_TK)
    start = pl.multiple_of(start, (NA_WIN_ROWS // 2) * GRID_W)
    hd = NA_HEAD_DIM
    q = q_ref[...].astype(F32).reshape(NA_Q_ROWS, GRID_W, hd)
    k = k_ref[pl.ds(start, NA_TK), :].astype(F32).reshape(NA_K_ROWS, GRID_W, hd)
    v = v_ref[pl.ds(start, NA_TK), :].astype(F32).reshape(NA_K_ROWS, GRID_W, hd)
    outs = []
    for cq in range(NA_N_CHUNKS):
        ks = NA_KEY_COL_START[cq]
        qc = q[:, cq * NA_CQ:(cq + 1) * NA_CQ, :].reshape(NA_CHUNK_Q, hd).astype(BF16)
        kc = k[:, ks:ks + NA_CK, :].reshape(NA_CHUNK_K, hd).astype(BF16)
        vc = v[:, ks:ks + NA_CK, :].reshape(NA_CHUNK_K, hd).astype(BF16)
        s = lax.dot_general(qc, kc, (((1,), (1,)), ((), ())), preferred_element_type=F32)
        s = s * (hd ** -0.5) + bias_ref[cq]
        m = jnp.max(s, axis=-1, keepdims=True)
        p = jnp.exp(s - m)
        l = jnp.sum(p, axis=-1, keepdims=True)
        o = jnp.dot(p.astype(BF16), vc, preferred_element_type=F32) / l
        outs.append(o.reshape(NA_Q_ROWS, NA_CQ, hd))
    o_ref[...] = jnp.concatenate(outs, axis=1).reshape(NA_TQ, hd).astype(o_ref.dtype)


def _neighbourhood_attention(z3, bias):
    b, seq, _ = z3.shape
    n_steps = seq // NA_TQ
    hd = NA_HEAD_DIM

    def kind(s):
        return jnp.where(s == 0, 0, jnp.where(s == n_steps - 1, 2, 1))

    bias_bytes = NA_N_CHUNKS * NA_CHUNK_Q * NA_CHUNK_K * 4
    blocks = NA_TQ * hd * 2 * 2 + 2 * seq * hd * 2 + bias_bytes + 3 * NA_TK * hd * 4 + 2 * bias_bytes
    return pl.pallas_call(
        functools.partial(_na_kernel, seq=seq),
        out_shape=jax.ShapeDtypeStruct((b, seq, NA_WIDTH), BF16),
        grid=(NA_HEADS, b, n_steps),
        in_specs=[
            pl.BlockSpec((None, NA_TQ, hd), lambda h, bi, s: (bi, s, OFF_NA_Q // hd + h)),
            pl.BlockSpec((None, seq, hd), lambda h, bi, s: (bi, 0, OFF_NA_K // hd + h)),
            pl.BlockSpec((None, seq, hd), lambda h, bi, s: (bi, 0, OFF_NA_V // hd + h)),
            pl.BlockSpec((None, None, NA_N_CHUNKS, NA_CHUNK_Q, NA_CHUNK_K),
                         lambda h, bi, s: (kind(s), h, 0, 0, 0)),
        ],
        out_specs=pl.BlockSpec((None, NA_TQ, hd), lambda h, bi, s: (bi, s, h)),
        compiler_params=_compiler_params(("parallel", "parallel", "arbitrary"), blocks),
        name="neighbourhood_attention",
    )(z3, z3, z3, bias)


RET_CHUNK = 256
RET_TB = 1024
RET_NC = RET_TB // RET_CHUNK


def _rope_tables(seq):
    half = RET_QK_DIM // 2
    inv_freq = ROPE_BASE ** (-jnp.arange(half, dtype=F32) / half)
    ang = jnp.arange(seq, dtype=F32)[:, None] * inv_freq[None, :]
    cos, sin = jnp.cos(ang), jnp.sin(ang)
    return jnp.concatenate([cos, cos], axis=-1), jnp.concatenate([-sin, sin], axis=-1)


def _decay_tables(decay, backward):
    c = RET_CHUNK
    lg = jax.nn.log_sigmoid(decay.astype(F32))
    pos = jnp.arange(c, dtype=F32)
    diff = pos[:, None] - pos[None, :]
    lg3 = lg[:, None, None]
    if backward:
        decay_in = jnp.exp(jnp.where(diff < 0, -diff * lg3, -jnp.inf))
        q_dec = jnp.exp((c - pos)[None, :] * lg[:, None])
        k_dec = jnp.exp(pos[None, :] * lg[:, None])
    else:
        decay_in = jnp.exp(jnp.where(diff >= 0, diff * lg3, -jnp.inf))
        q_dec = jnp.exp((pos + 1.0)[None, :] * lg[:, None])
        k_dec = jnp.exp((c - 1.0 - pos)[None, :] * lg[:, None])
    chunk_dec = jnp.exp(c * lg)
    q_dec = jnp.broadcast_to(q_dec[:, :, None], (RET_HEADS, c, RET_QK_DIM))
    k_dec_t = jnp.broadcast_to(k_dec[:, None, :], (RET_HEADS, RET_QK_DIM, c))
    return decay_in, q_dec, k_dec_t, chunk_dec


def _retention_kernel(cd_ref, q_ref, k_ref, v_ref, cos_ref, sin_ref, din_ref, qd_ref, kdt_ref, g_ref,
                      o_ref, state_ref, ofwd_ref, *, n_blk):
    h = pl.program_id(1)
    direction = pl.program_id(2)
    j = pl.program_id(3)

    @pl.when(j == 0)
    def _():
        state_ref[...] = jnp.zeros_like(state_ref)

    cd = cd_ref[direction, h]
    cos, sin = cos_ref[...], sin_ref[...]
    q = q_ref[...].astype(F32)
    k = k_ref[...].astype(F32)
    half = RET_QK_DIM // 2
    q = q * cos + pltpu.roll(q, half, 1) * sin
    k = (k * cos + pltpu.roll(k, half, 1) * sin) * (RET_QK_DIM ** -0.5)
    kt = k.T
    din, qd, kdt = din_ref[...], qd_ref[...], kdt_ref[...]

    def scan(order, emit):
        state = state_ref[...]
        for c in order:
            sl = slice(c * RET_CHUNK, (c + 1) * RET_CHUNK)
            qc = q[sl]
            ktc = kt[:, sl]
            vc = v_ref[sl, :]
            scores = jnp.dot(qc.astype(BF16), ktc.astype(BF16), preferred_element_type=F32) * din
            out = (jnp.dot(scores.astype(BF16), vc, preferred_element_type=F32)
                   + jnp.dot((qc * qd).astype(BF16), state.astype(BF16), preferred_element_type=F32))
            state = cd * state + jnp.dot((ktc * kdt).astype(BF16), vc, preferred_element_type=F32)
            emit(c, sl, out)
        state_ref[...] = state

    @pl.when(direction == 0)
    def _():
        def emit(c, sl, out):
            row = pl.multiple_of(j * RET_TB + c * RET_CHUNK, RET_CHUNK)
            ofwd_ref[pl.ds(row, RET_CHUNK), :] = out

        scan(range(RET_NC), emit)

    @pl.when(direction == 1)
    def _():
        def emit(c, sl, out):
            row = pl.multiple_of((n_blk - 1 - j) * RET_TB + c * RET_CHUNK, RET_CHUNK)
            o = ofwd_ref[pl.ds(row, RET_CHUNK), :] + out
            o = o * lax.rsqrt(jnp.mean(o * o, axis=-1, keepdims=True) + EPS)
            g = g_ref[sl, :].astype(F32)
            o_ref[sl, :] = (o * (g * jax.nn.sigmoid(g))).astype(o_ref.dtype)

        scan(range(RET_NC - 1, -1, -1), emit)


def _retention(z3, rope, tables):
    b, seq, _ = z3.shape
    n_blk = seq // RET_TB
    dk, dv, c = RET_QK_DIM, RET_V_DIM, RET_CHUNK
    cos2, sin2 = rope
    din, qd, kdt, cd = tables

    def tok(d, j):
        return jnp.where(d == 0, j, n_blk - 1 - j)

    def out_blk(d, j):
        return jnp.where(d == 0, n_blk - 1, n_blk - 1 - j)

    blocks = RET_TB * (2 * dk * 2 + dv * 2 + 2 * dk * 4 + 2 * dv * 2) + (c * c + 2 * c * dk) * 4
    scratch_bytes = dk * dv * 4 + seq * dv * 4
    temporaries = 6 * RET_TB * dk * 4 + 4 * c * c * 4
    params = pltpu.CompilerParams(
        dimension_semantics=("parallel", "parallel", "arbitrary", "arbitrary"),
        vmem_limit_bytes=int(2 * blocks + scratch_bytes + temporaries + (8 << 20)))
    return pl.pallas_call(
        functools.partial(_retention_kernel, n_blk=n_blk),
        out_shape=jax.ShapeDtypeStruct((b, seq, RET_V_WIDTH), BF16),
        grid=(b, RET_HEADS, 2, n_blk),
        in_specs=[
            pl.BlockSpec(memory_space=pltpu.SMEM),
            pl.BlockSpec((None, RET_TB, dk), lambda bi, h, d, j: (bi, tok(d, j), OFF_RET_Q // dk + h)),
            pl.BlockSpec((None, RET_TB, dk), lambda bi, h, d, j: (bi, tok(d, j), OFF_RET_K // dk + h)),
            pl.BlockSpec((None, RET_TB, dv), lambda bi, h, d, j: (bi, tok(d, j), OFF_RET_V // dv + h)),
            pl.BlockSpec((RET_TB, dk), lambda bi, h, d, j: (tok(d, j), 0)),
            pl.BlockSpec((RET_TB, dk), lambda bi, h, d, j: (tok(d, j), 0)),
            pl.BlockSpec((None, None, c, c), lambda bi, h, d, j: (d, h, 0, 0)),
            pl.BlockSpec((None, None, c, dk), lambda bi, h, d, j: (d, h, 0, 0)),
            pl.BlockSpec((None, None, dk, c), lambda bi, h, d, j: (d, h, 0, 0)),
            pl.BlockSpec((None, RET_TB, dv), lambda bi, h, d, j: (bi, out_blk(d, j), OFF_RET_G // dv + h)),
        ],
        out_specs=pl.BlockSpec((None, RET_TB, dv), lambda bi, h, d, j: (bi, out_blk(d, j), h)),
        scratch_shapes=[pltpu.VMEM((dk, dv), F32), pltpu.VMEM((seq, dv), F32)],
        compiler_params=params,
        name="retention",
    )(cd, z3, z3, z3, cos2, sin2, din, qd, kdt, z3)


MIX_TM = 1024
MIX_TN = 1024


def _mix_kernel(a_ref, r_ref, wna_ref, wret_ref, ga_ref, gb_ref, m_ref):
    a = jnp.dot(a_ref[...], wna_ref[...], preferred_element_type=F32)
    r = jnp.dot(r_ref[...], wret_ref[...], preferred_element_type=F32)
    ga = jax.nn.sigmoid(ga_ref[...].astype(F32))
    gb = jax.nn.sigmoid(gb_ref[...].astype(F32))
    m_ref[...] = (ga * a + gb * r).astype(m_ref.dtype)


def _branch_mix(a2d, r2d, z2d, w_na_out, w_ret_out):
    m = a2d.shape[0]
    tm, tn = MIX_TM, MIX_TN
    blocks = tm * NA_WIDTH * 2 + tm * RET_V_WIDTH * 2 + (NA_WIDTH + RET_V_WIDTH) * tn * 2 + 3 * tm * tn * 2
    blocks += 4 * tm * tn * 4
    return pl.pallas_call(
        _mix_kernel,
        out_shape=jax.ShapeDtypeStruct((m, D_MODEL), BF16),
        grid=(m // tm, D_MODEL // tn),
        in_specs=[
            pl.BlockSpec((tm, NA_WIDTH), lambda i, j: (i, 0)),
            pl.BlockSpec((tm, RET_V_WIDTH), lambda i, j: (i, 0)),
            pl.BlockSpec((NA_WIDTH, tn), lambda i, j: (0, j)),
            pl.BlockSpec((RET_V_WIDTH, tn), lambda i, j: (0, j)),
            pl.BlockSpec((tm, tn), lambda i, j: (i, OFF_GATE_A // tn + j)),
            pl.BlockSpec((tm, tn), lambda i, j: (i, OFF_GATE_B // tn + j)),
        ],
        out_specs=pl.BlockSpec((tm, tn), lambda i, j: (i, j)),
        compiler_params=_compiler_params(("parallel", "arbitrary"), blocks),
        name="branch_mix",
    )(a2d, r2d, w_na_out, w_ret_out, z2d, z2d)


OUT_TM = 1024
OUT_TN = 1024


def _outproj_kernel(x_ref, m_ref, w_ref, h_ref):
    h_ref[...] = x_ref[...] + jnp.dot(m_ref[...], w_ref[...], preferred_element_type=F32)


def _outproj(x2d, m2d, w_out):
    m = x2d.shape[0]
    tm, tn = OUT_TM, OUT_TN
    blocks = 2 * tm * tn * 4 + tm * D_MODEL * 2 + D_MODEL * tn * 2 + tm * tn * 4
    return pl.pallas_call(
        _outproj_kernel,
        out_shape=jax.ShapeDtypeStruct((m, D_MODEL), F32),
        grid=(m // tm, D_MODEL // tn),
        in_specs=[
            pl.BlockSpec((tm, tn), lambda i, j: (i, j)),
            pl.BlockSpec((tm, D_MODEL), lambda i, j: (i, 0)),
            pl.BlockSpec((D_MODEL, tn), lambda i, j: (0, j)),
        ],
        out_specs=pl.BlockSpec((tm, tn), lambda i, j: (i, j)),
        compiler_params=_compiler_params(("parallel", "arbitrary"), blocks),
        name="outproj_residual",
    )(x2d, m2d, w_out)


MLP_TM = 512
MLP_TF = 1024


def _mlp_kernel(h_ref, nw_ref, wup_ref, wdown_ref, fw_ref, y_ref, hn_ref, acc_ref):
    f = pl.program_id(1)

    @pl.when(f == 0)
    def _():
        h = h_ref[...]
        ms = jnp.mean(h * h, axis=-1, keepdims=True)
        hn_ref[...] = (h * lax.rsqrt(ms + EPS) * nw_ref[...]).astype(BF16)
        acc_ref[...] = jnp.zeros_like(acc_ref)

    u = jnp.maximum(jnp.dot(hn_ref[...], wup_ref[...], preferred_element_type=F32), 0.0)
    acc_ref[...] += jnp.dot((u * u).astype(BF16), wdown_ref[...], preferred_element_type=F32)

    @pl.when(f == pl.num_programs(1) - 1)
    def _():
        x = h_ref[...] + acc_ref[...]
        ms = jnp.mean(x * x, axis=-1, keepdims=True)
        y_ref[...] = x * lax.rsqrt(ms + EPS) * fw_ref[...]


def _mlp(h2d, norm_w, w_up, w_down, final_w):
    m = h2d.shape[0]
    tm, tf = MLP_TM, MLP_TF
    blocks = 2 * tm * D_MODEL * 4 + 2 * D_MODEL * tf * 2 + tm * D_MODEL * 2 + tm * D_MODEL * 4 + 2 * tm * tf * 4
    return pl.pallas_call(
        _mlp_kernel,
        out_shape=jax.ShapeDtypeStruct((m, D_MODEL), F32),
        grid=(m // tm, D_FF // tf),
        in_specs=[
            pl.BlockSpec((tm, D_MODEL), lambda i, f: (i, 0)),
            pl.BlockSpec((1, D_MODEL), lambda i, f: (0, 0)),
            pl.BlockSpec((D_MODEL, tf), lambda i, f: (0, f)),
            pl.BlockSpec((tf, D_MODEL), lambda i, f: (f, 0)),
            pl.BlockSpec((1, D_MODEL), lambda i, f: (0, 0)),
        ],
        out_specs=pl.BlockSpec((tm, D_MODEL), lambda i, f: (i, 0)),
        scratch_shapes=[pltpu.VMEM((tm, D_MODEL), BF16), pltpu.VMEM((tm, D_MODEL), F32)],
        compiler_params=_compiler_params(("parallel", "arbitrary"), blocks),
        name="mlp_final_norm",
    )(h2d, norm_w.reshape(1, D_MODEL), w_up, w_down, final_w.reshape(1, D_MODEL))


def _encode(x, p):
    b, seq, d = x.shape
    x2d = x.reshape(b * seq, d)
    z2d = _inproj(x2d, p["norm_mix_w"], p["w_in"])
    z3 = z2d.reshape(b, seq, IN_COLS)
    a = _neighbourhood_attention(z3, p["na_bias"])
    r = _retention(z3, p["rope"], p["ret_tables"])
    m2d = _branch_mix(a.reshape(b * seq, NA_WIDTH), r.reshape(b * seq, RET_V_WIDTH), z2d,
                      p["w_na_out"], p["w_ret_out"])
    h2d = _outproj(x2d, m2d, p["w_out"])
    y2d = _mlp(h2d, p["norm_mlp_w"], p["w_mlp_up"], p["w_mlp_down"], p["norm_final_w"])
    return y2d.reshape(b, seq, d)


def kernel(x_prompt, x_sample, norm_mix_w, w_in, na_rpb, ret_decay_fwd, ret_decay_bwd, w_na_out, w_ret_out,
           w_out, norm_mlp_w, w_mlp_up, w_mlp_down, norm_final_w):
    depth = w_in.shape[0]
    seq = x_prompt.shape[1]
    assert x_sample.shape[1] == seq and seq % RET_TB == 0 and seq % NA_TQ == 0
    assert depth == 1, "the final norm is fused into the last layer's MLP kernel"
    rope = _rope_tables(seq)
    l = 0
    fwd = _decay_tables(ret_decay_fwd[l], False)
    bwd = _decay_tables(ret_decay_bwd[l], True)
    params = {
        "norm_mix_w": norm_mix_w[l].astype(F32),
        "w_in": w_in[l].astype(BF16),
        "na_bias": _na_bias_table(na_rpb[l], seq // GRID_W),
        "rope": rope,
        "ret_tables": tuple(jnp.stack([f, b]) for f, b in zip(fwd, bwd)),
        "w_na_out": w_na_out[l].astype(BF16),
        "w_ret_out": w_ret_out[l].astype(BF16),
        "w_out": w_out[l].astype(BF16),
        "norm_mlp_w": norm_mlp_w[l].astype(F32),
        "w_mlp_up": w_mlp_up[l].astype(BF16),
        "w_mlp_down": w_mlp_down[l].astype(BF16),
        "norm_final_w": norm_final_w.astype(F32),
    }
    return _encode(x_prompt, params), _encode(x_sample, params)
```

```python
import functools

import jax
import jax.numpy as jnp
import numpy as np
from jax import lax
from jax.experimental import pallas as pl
from jax.experimental.pallas import tpu as pltpu

F32 = jnp.float32
BF16 = jnp.bfloat16

D_MODEL = 2048
GRID_W = 64
NA_HEADS = 8
NA_HEAD_DIM = 128
NA_WIDTH = NA_HEADS * NA_HEAD_DIM
NA_WIN_ROWS = 8
NA_WIN_COLS = 16
RET_HEADS = 8
RET_QK_DIM = 128
RET_V_DIM = 256
RET_QK_WIDTH = RET_HEADS * RET_QK_DIM
RET_V_WIDTH = RET_HEADS * RET_V_DIM
ROPE_BASE = 10000.0
D_FF = 4 * D_MODEL
EPS = 1e-6

OFF_NA_Q = 0
OFF_NA_K = OFF_NA_Q + NA_WIDTH
OFF_NA_V = OFF_NA_K + NA_WIDTH
OFF_RET_Q = OFF_NA_V + NA_WIDTH
OFF_RET_K = OFF_RET_Q + RET_QK_WIDTH
OFF_RET_V = OFF_RET_K + RET_QK_WIDTH
OFF_RET_G = OFF_RET_V + RET_V_WIDTH
OFF_GATE_A = OFF_RET_G + RET_V_WIDTH
OFF_GATE_B = OFF_GATE_A + D_MODEL
IN_COLS = OFF_GATE_B + D_MODEL

MASK_VALUE = -1e30
V7X_VMEM_BYTES = 64 * 1024 * 1024


def _compiler_params(semantics, block_bytes):
    limit = min(2 * block_bytes + (8 << 20), V7X_VMEM_BYTES - (4 << 20))
    return pltpu.CompilerParams(dimension_semantics=semantics, vmem_limit_bytes=int(limit))


IN_TM = 1024
IN_TN = 1024


def _inproj_kernel(x_ref, nw_ref, w_ref, z_ref, xn_ref):
    @pl.when(pl.program_id(1) == 0)
    def _():
        x = x_ref[...]
        ms = jnp.mean(x * x, axis=-1, keepdims=True)
        xn_ref[...] = (x * lax.rsqrt(ms + EPS) * nw_ref[...]).astype(BF16)

    z_ref[...] = jnp.dot(xn_ref[...], w_ref[...], preferred_element_type=F32).astype(z_ref.dtype)


def _inproj(x2d, norm_w, w_in):
    m = x2d.shape[0]
    tm, tn = IN_TM, IN_TN
    blocks = tm * D_MODEL * 4 + D_MODEL * tn * 2 + tm * tn * 2 + tm * D_MODEL * 2
    return pl.pallas_call(
        _inproj_kernel,
        out_shape=jax.ShapeDtypeStruct((m, IN_COLS), BF16),
        grid=(m // tm, IN_COLS // tn),
        in_specs=[
            pl.BlockSpec((tm, D_MODEL), lambda i, j: (i, 0)),
            pl.BlockSpec((1, D_MODEL), lambda i, j: (0, 0)),
            pl.BlockSpec((D_MODEL, tn), lambda i, j: (0, j)),
        ],
        out_specs=pl.BlockSpec((tm, tn), lambda i, j: (i, j)),
        scratch_shapes=[pltpu.VMEM((tm, D_MODEL), BF16)],
        compiler_params=_compiler_params(("parallel", "arbitrary"), blocks),
        name="norm_inproj",
    )(x2d, norm_w.reshape(1, D_MODEL), w_in)


NA_Q_ROWS = 8
NA_K_ROWS = 16
NA_TQ = NA_Q_ROWS * GRID_W
NA_TK = NA_K_ROWS * GRID_W
NA_CQ = 16
NA_CK = 32
NA_N_CHUNKS = GRID_W // NA_CQ
NA_CHUNK_Q = NA_Q_ROWS * NA_CQ
NA_CHUNK_K = NA_K_ROWS * NA_CK
SUBLANES = 8


def _na_key_col_starts():
    starts = []
    for cq in range(NA_N_CHUNKS):
        c = np.arange(cq * NA_CQ, (cq + 1) * NA_CQ)
        c0 = np.clip(c - NA_WIN_COLS // 2, 0, GRID_W - NA_WIN_COLS)
        start = min((int(c0.min()) // SUBLANES) * SUBLANES, GRID_W - NA_CK)
        assert start <= c0.min() and c0.max() + NA_WIN_COLS <= start + NA_CK
        starts.append(start)
    return tuple(starts)


NA_KEY_COL_START = _na_key_col_starts()


def _na_bias_table(rpb, rows):
    cl = np.arange(NA_CQ)
    kcl = np.arange(NA_CK)
    c = (np.arange(NA_N_CHUNKS) * NA_CQ)[:, None, None] + cl[None, :, None]
    kc = np.asarray(NA_KEY_COL_START)[:, None, None] + kcl[None, None, :]
    c0 = np.clip(c - NA_WIN_COLS // 2, 0, GRID_W - NA_WIN_COLS)
    col_valid = (kc >= c0) & (kc < c0 + NA_WIN_COLS)
    col_idx = np.clip(kc - c + NA_WIN_COLS - 1, 0, 2 * NA_WIN_COLS - 2)
    tc = jnp.where(col_valid[None, None], rpb.astype(F32)[:, :, col_idx], MASK_VALUE)
    invalid = 2 * NA_WIN_ROWS - 1
    tc = jnp.concatenate([tc, jnp.full((NA_HEADS, 1) + tc.shape[2:], MASK_VALUE, F32)], axis=1)

    ri = np.arange(NA_Q_ROWS)
    kri = np.arange(NA_K_ROWS)
    half = NA_WIN_ROWS // 2
    n_steps = rows // NA_Q_ROWS
    row_idx = []
    for kind in range(3):
        q_base = {0: 0, 1: NA_Q_ROWS, 2: (n_steps - 1) * NA_Q_ROWS}[kind]
        k_base = int(np.clip(q_base - half, 0, rows - NA_K_ROWS))
        r = q_base + ri
        r0 = np.clip(r - half, 0, rows - NA_WIN_ROWS)
        kr = k_base + kri
        valid = (kr[None, :] >= r0[:, None]) & (kr[None, :] < r0[:, None] + NA_WIN_ROWS)
        idx = kr[None, :] - r[:, None] + NA_WIN_ROWS - 1
        row_idx.append(np.where(valid, idx, invalid))
    row_idx = np.stack(row_idx)
    bias = tc[:, row_idx]
    bias = jnp.transpose(bias, (1, 0, 4, 2, 5, 3, 6))
    return bias.reshape(3, NA_HEADS, NA_N_CHUNKS, NA_CHUNK_Q, NA_CHUNK_K)


def _na_kernel(q_ref, k_ref, v_ref, bias_ref, o_ref, *, seq):
    step = pl.program_id(2)
    start = jnp.clip(step * NA_TQ - (NA_WIN_ROWS // 2) * GRID_W, 0, seq - NA_TK)
    start = pl.multiple_of(start, (NA_WIN_ROWS // 2) * GRID_W)
    hd = NA_HEAD_DIM
    q = q_ref[...].astype(F32).reshape(NA_Q_ROWS, GRID_W, hd)
    k = k_ref[pl.ds(start, NA_TK), :].astype(F32).reshape(NA_K_ROWS, GRID_W, hd)
    v = v_ref[pl.ds(start, NA_TK), :].astype(F32).reshape(NA_K_ROWS, GRID_W, hd)
    chunks = range(NA_N_CHUNKS)
    scores = []
    for cq in chunks:
        ks = NA_KEY_COL_START[cq]
        qc = q[:, cq * NA_CQ:(cq + 1) * NA_CQ, :].reshape(NA_CHUNK_Q, hd).astype(BF16)
        kc = k[:, ks:ks + NA_CK, :].reshape(NA_CHUNK_K, hd).astype(BF16)
        scores.append(lax.dot_general(qc, kc, (((1,), (1,)), ((), ())), preferred_element_type=F32))
    probs, denoms = [], []
    for cq in chunks:
        s = scores[cq] * (hd ** -0.5) + bias_ref[cq]
        p = jnp.exp(s - jnp.max(s, axis=-1, keepdims=True))
        denoms.append(jnp.sum(p, axis=-1, keepdims=True))
        probs.append(p.astype(BF16))
    outs = []
    for cq in chunks:
        ks = NA_KEY_COL_START[cq]
        vc = v[:, ks:ks + NA_CK, :].reshape(NA_CHUNK_K, hd).astype(BF16)
        o = jnp.dot(probs[cq], vc, preferred_element_type=F32) / denoms[cq]
        outs.append(o.reshape(NA_Q_ROWS, NA_CQ, hd))
    o_ref[...] = jnp.concatenate(outs, axis=1).reshape(NA_TQ, hd).astype(o_ref.dtype)


def _neighbourhood_attention(z3, bias):
    b, seq, _ = z3.shape
    n_steps = seq // NA_TQ
    hd = NA_HEAD_DIM

    def kind(s):
        return jnp.where(s == 0, 0, jnp.where(s == n_steps - 1, 2, 1))

    bias_bytes = NA_N_CHUNKS * NA_CHUNK_Q * NA_CHUNK_K * 4
    blocks = NA_TQ * hd * 2 * 2 + 2 * seq * hd * 2 + bias_bytes + 3 * NA_TK * hd * 4 + 2 * bias_bytes
    return pl.pallas_call(
        functools.partial(_na_kernel, seq=seq),
        out_shape=jax.ShapeDtypeStruct((b, seq, NA_WIDTH), BF16),
        grid=(NA_HEADS, b, n_steps),
        in_specs=[
            pl.BlockSpec((None, NA_TQ, hd), lambda h, bi, s: (bi, s, OFF_NA_Q // hd + h)),
            pl.BlockSpec((None, seq, hd), lambda h, bi, s: (bi, 0, OFF_NA_K // hd + h)),
            pl.BlockSpec((None, seq, hd), lambda h, bi, s: (bi, 0, OFF_NA_V // hd + h)),
            pl.BlockSpec((None, None, NA_N_CHUNKS, NA_CHUNK_Q, NA_CHUNK_K),
                         lambda h, bi, s: (kind(s), h, 0, 0, 0)),
        ],
        out_specs=pl.BlockSpec((None, NA_TQ, hd), lambda h, bi, s: (bi, s, h)),
        compiler_params=_compiler_params(("parallel", "parallel", "arbitrary"), blocks),
        name="neighbourhood_attention",
    )(z3, z3, z3, bias)


RET_CHUNK = 256
RET_TB = 1024
RET_NC = RET_TB // RET_CHUNK


def _rope_tables(seq):
    half = RET_QK_DIM // 2
    inv_freq = ROPE_BASE ** (-jnp.arange(half, dtype=F32) / half)
    ang = jnp.arange(seq, dtype=F32)[:, None] * inv_freq[None, :]
    cos, sin = jnp.cos(ang), jnp.sin(ang)
    return jnp.concatenate([cos, cos], axis=-1), jnp.concatenate([-sin, sin], axis=-1)


def _decay_tables(decay, backward):
    c = RET_CHUNK
    lg = jax.nn.log_sigmoid(decay.astype(F32))
    pos = jnp.arange(c, dtype=F32)
    diff = pos[:, None] - pos[None, :]
    lg3 = lg[:, None, None]
    if backward:
        decay_in = jnp.exp(jnp.where(diff < 0, -diff * lg3, -jnp.inf))
        q_dec = jnp.exp((c - pos)[None, :] * lg[:, None])
        k_dec = jnp.exp(pos[None, :] * lg[:, None])
    else:
        decay_in = jnp.exp(jnp.where(diff >= 0, diff * lg3, -jnp.inf))
        q_dec = jnp.exp((pos + 1.0)[None, :] * lg[:, None])
        k_dec = jnp.exp((c - 1.0 - pos)[None, :] * lg[:, None])
    chunk_dec = jnp.exp(c * lg)
    return decay_in, q_dec, k_dec, chunk_dec


def _retention_tables(decay_fwd, decay_bwd):
    c, dk = RET_CHUNK, RET_QK_DIM
    din_f, qd_f, kd_f, cd_f = _decay_tables(decay_fwd, False)
    din_b, qd_b, kd_b, cd_b = _decay_tables(decay_bwd, True)
    qd2 = jnp.concatenate([jnp.broadcast_to(qd_f[:, :, None], (RET_HEADS, c, dk)),
                           jnp.broadcast_to(qd_b[:, :, None], (RET_HEADS, c, dk))], axis=2)
    kdt2 = jnp.concatenate([jnp.broadcast_to(kd_f[:, None, :], (RET_HEADS, dk, c)),
                            jnp.broadcast_to(kd_b[:, None, :], (RET_HEADS, dk, c))], axis=1)
    return din_f + din_b, qd2, kdt2, jnp.stack([cd_f, cd_b])


def _retention_kernel(cd_ref, q_ref, k_ref, v_ref, cos_ref, sin_ref, din_ref, qd_ref, kdt_ref, g_ref,
                      o_ref, statef_ref, stateb_ref, kt_ref, kvf_ref, sb_ref, *, n_blk):
    h = pl.program_id(1)
    pass_id = pl.program_id(2)
    j = pl.program_id(3)
    dk, half = RET_QK_DIM, RET_QK_DIM // 2
    chunks = range(RET_NC)
    slices = [slice(c * RET_CHUNK, (c + 1) * RET_CHUNK) for c in chunks]

    @pl.when(j == 0)
    def _():
        statef_ref[...] = jnp.zeros_like(statef_ref)
        stateb_ref[...] = jnp.zeros_like(stateb_ref)

    def rotate(ref):
        x = ref[...].astype(F32)
        return x * cos_ref[...] + pltpu.roll(x, half, 1) * sin_ref[...]

    @pl.when(pass_id == 0)
    def _():
        blk = n_blk - 1 - j
        kt = (rotate(k_ref) * (dk ** -0.5)).T
        kt_ref[blk] = kt.astype(BF16)
        kdt2 = kdt_ref[...]
        kv = [jnp.dot((jnp.concatenate([kt[:, sl], kt[:, sl]], axis=0) * kdt2).astype(BF16), v_ref[sl, :],
                      preferred_element_type=F32) for sl in slices]
        cd_b = cd_ref[1, h]
        state = stateb_ref[...]
        for c in reversed(chunks):
            sb_ref[blk * RET_NC + c] = state.astype(BF16)
            kvf_ref[blk * RET_NC + c] = kv[c][:dk]
            state = cd_b * state + kv[c][dk:]
        stateb_ref[...] = state

    @pl.when(pass_id == 1)
    def _():
        blk = j
        q = rotate(q_ref)
        kt = kt_ref[blk]
        din, qd2 = din_ref[...], qd_ref[...]
        scores = [jnp.dot(q[sl].astype(BF16), kt[:, sl], preferred_element_type=F32) for sl in slices]
        intra = [jnp.dot((s * din).astype(BF16), v_ref[sl, :], preferred_element_type=F32)
                 for s, sl in zip(scores, slices)]
        cd_f = cd_ref[0, h]
        states = [statef_ref[...]]
        for c in chunks:
            states.append(cd_f * states[-1] + kvf_ref[blk * RET_NC + c])
        statef_ref[...] = states[-1]
        for c, sl in zip(chunks, slices):
            lhs = (jnp.concatenate([q[sl], q[sl]], axis=1) * qd2).astype(BF16)
            rhs = jnp.concatenate([states[c].astype(BF16), sb_ref[blk * RET_NC + c]], axis=0)
            o = intra[c] + jnp.dot(lhs, rhs, preferred_element_type=F32)
            o = o * lax.rsqrt(jnp.mean(o * o, axis=-1, keepdims=True) + EPS)
            g = g_ref[sl, :].astype(F32)
            o_ref[sl, :] = (o * (g * jax.nn.sigmoid(g))).astype(o_ref.dtype)


def _retention(z3, rope, tables):
    b, seq, _ = z3.shape
    n_blk = seq // RET_TB
    dk, dv, c = RET_QK_DIM, RET_V_DIM, RET_CHUNK
    cos2, sin2 = rope
    din, qd, kdt, cd = tables

    def tok(p, j):
        return jnp.where(p == 0, n_blk - 1 - j, j)

    def pass1_blk(p, j):
        return jnp.where(p == 0, 0, j)

    def pass0_blk(p, j):
        return jnp.where(p == 0, n_blk - 1 - j, 0)

    n_chunks = seq // c
    blocks = RET_TB * (2 * dk * 2 + dv * 2 + 2 * dk * 4 + 2 * dv * 2) + 3 * c * c * 4
    scratch_bytes = 2 * dk * dv * 4 + seq * dk * 2 + n_chunks * dk * dv * (4 + 2)
    temporaries = 6 * RET_TB * dk * 4 + 16 * c * c * 4
    params = pltpu.CompilerParams(
        dimension_semantics=("parallel", "parallel", "arbitrary", "arbitrary"),
        vmem_limit_bytes=int(2 * blocks + scratch_bytes + temporaries + (8 << 20)))
    return pl.pallas_call(
        functools.partial(_retention_kernel, n_blk=n_blk),
        out_shape=jax.ShapeDtypeStruct((b, seq, RET_V_WIDTH), BF16),
        grid=(b, RET_HEADS, 2, n_blk),
        in_specs=[
            pl.BlockSpec(memory_space=pltpu.SMEM),
            pl.BlockSpec((None, RET_TB, dk), lambda bi, h, p, j: (bi, pass1_blk(p, j), OFF_RET_Q // dk + h)),
            pl.BlockSpec((None, RET_TB, dk), lambda bi, h, p, j: (bi, pass0_blk(p, j), OFF_RET_K // dk + h)),
            pl.BlockSpec((None, RET_TB, dv), lambda bi, h, p, j: (bi, tok(p, j), OFF_RET_V // dv + h)),
            pl.BlockSpec((RET_TB, dk), lambda bi, h, p, j: (tok(p, j), 0)),
            pl.BlockSpec((RET_TB, dk), lambda bi, h, p, j: (tok(p, j), 0)),
            pl.BlockSpec((None, c, c), lambda bi, h, p, j: (h, 0, 0)),
            pl.BlockSpec((None, c, 2 * dk), lambda bi, h, p, j: (h, 0, 0)),
            pl.BlockSpec((None, 2 * dk, c), lambda bi, h, p, j: (h, 0, 0)),
            pl.BlockSpec((None, RET_TB, dv), lambda bi, h, p, j: (bi, pass1_blk(p, j), OFF_RET_G // dv + h)),
        ],
        out_specs=pl.BlockSpec((None, RET_TB, dv), lambda bi, h, p, j: (bi, pass1_blk(p, j), h)),
        scratch_shapes=[
            pltpu.VMEM((dk, dv), F32),
            pltpu.VMEM((dk, dv), F32),
            pltpu.VMEM((n_blk, dk, RET_TB), BF16),
            pltpu.VMEM((n_chunks, dk, dv), F32),
            pltpu.VMEM((n_chunks, dk, dv), BF16),
        ],
        compiler_params=params,
        name="retention",
    )(cd, z3, z3, z3, cos2, sin2, din, qd, kdt, z3)


MIX_TM = 1024
MIX_TN = 1024


def _mix_kernel(a_ref, r_ref, wna_ref, wret_ref, ga_ref, gb_ref, m_ref):
    a = jnp.dot(a_ref[...], wna_ref[...], preferred_element_type=F32)
    r = jnp.dot(r_ref[...], wret_ref[...], preferred_element_type=F32)
    ga = jax.nn.sigmoid(ga_ref[...].astype(F32))
    gb = jax.nn.sigmoid(gb_ref[...].astype(F32))
    m_ref[...] = (ga * a + gb * r).astype(m_ref.dtype)


def _branch_mix(a2d, r2d, z2d, w_na_out, w_ret_out):
    m = a2d.shape[0]
    tm, tn = MIX_TM, MIX_TN
    blocks = tm * NA_WIDTH * 2 + tm * RET_V_WIDTH * 2 + (NA_WIDTH + RET_V_WIDTH) * tn * 2 + 3 * tm * tn * 2
    blocks += 4 * tm * tn * 4
    return pl.pallas_call(
        _mix_kernel,
        out_shape=jax.ShapeDtypeStruct((m, D_MODEL), BF16),
        grid=(m // tm, D_MODEL // tn),
        in_specs=[
            pl.BlockSpec((tm, NA_WIDTH), lambda i, j: (i, 0)),
            pl.BlockSpec((tm, RET_V_WIDTH), lambda i, j: (i, 0)),
            pl.BlockSpec((NA_WIDTH, tn), lambda i, j: (0, j)),
            pl.BlockSpec((RET_V_WIDTH, tn), lambda i, j: (0, j)),
            pl.BlockSpec((tm, tn), lambda i, j: (i, OFF_GATE_A // tn + j)),
            pl.BlockSpec((tm, tn), lambda i, j: (i, OFF_GATE_B // tn + j)),
        ],
        out_specs=pl.BlockSpec((tm, tn), lambda i, j: (i, j)),
        compiler_params=_compiler_params(("parallel", "arbitrary"), blocks),
        name="branch_mix",
    )(a2d, r2d, w_na_out, w_ret_out, z2d, z2d)


OUT_TM = 1024
OUT_TN = 1024


def _outproj_kernel(x_ref, m_ref, w_ref, h_ref):
    h_ref[...] = x_ref[...] + jnp.dot(m_ref[...], w_ref[...], preferred_element_type=F32)


def _outproj(x2d, m2d, w_out):
    m = x2d.shape[0]
    tm, tn = OUT_TM, OUT_TN
    blocks = 2 * tm * tn * 4 + tm * D_MODEL * 2 + D_MODEL * tn * 2 + tm * tn * 4
    return pl.pallas_call(
        _outproj_kernel,
        out_shape=jax.ShapeDtypeStruct((m, D_MODEL), F32),
        grid=(m // tm, D_MODEL // tn),
        in_specs=[
            pl.BlockSpec((tm, tn), lambda i, j: (i, j)),
            pl.BlockSpec((tm, D_MODEL), lambda i, j: (i, 0)),
            pl.BlockSpec((D_MODEL, tn), lambda i, j: (0, j)),
        ],
        out_specs=pl.BlockSpec((tm, tn), lambda i, j: (i, j)),
        compiler_params=_compiler_params(("parallel", "arbitrary"), blocks),
        name="outproj_residual",
    )(x2d, m2d, w_out)


MLP_TM = 512
MLP_TF = 1024


def _mlp_kernel(h_ref, nw_ref, wup_ref, wdown_ref, fw_ref, y_ref, hn_ref, acc_ref):
    f = pl.program_id(1)

    @pl.when(f == 0)
    def _():
        h = h_ref[...]
        ms = jnp.mean(h * h, axis=-1, keepdims=True)
        hn_ref[...] = (h * lax.rsqrt(ms + EPS) * nw_ref[...]).astype(BF16)
        acc_ref[...] = jnp.zeros_like(acc_ref)

    u = jnp.maximum(jnp.dot(hn_ref[...], wup_ref[...], preferred_element_type=F32), 0.0)
    acc_ref[...] += jnp.dot((u * u).astype(BF16), wdown_ref[...], preferred_element_type=F32)

    @pl.when(f == pl.num_programs(1) - 1)
    def _():
        x = h_ref[...] + acc_ref[...]
        ms = jnp.mean(x * x, axis=-1, keepdims=True)
        y_ref[...] = x * lax.rsqrt(ms + EPS) * fw_ref[...]


def _mlp(h2d, norm_w, w_up, w_down, final_w):
    m = h2d.shape[0]
    tm, tf = MLP_TM, MLP_TF
    blocks = 2 * tm * D_MODEL * 4 + 2 * D_MODEL * tf * 2 + tm * D_MODEL * 2 + tm * D_MODEL * 4 + 2 * tm * tf * 4
    return pl.pallas_call(
        _mlp_kernel,
        out_shape=jax.ShapeDtypeStruct((m, D_MODEL), F32),
        grid=(m // tm, D_FF // tf),
        in_specs=[
            pl.BlockSpec((tm, D_MODEL), lambda i, f: (i, 0)),
            pl.BlockSpec((1, D_MODEL), lambda i, f: (0, 0)),
            pl.BlockSpec((D_MODEL, tf), lambda i, f: (0, f)),
            pl.BlockSpec((tf, D_MODEL), lambda i, f: (f, 0)),
            pl.BlockSpec((1, D_MODEL), lambda i, f: (0, 0)),
        ],
        out_specs=pl.BlockSpec((tm, D_MODEL), lambda i, f: (i, 0)),
        scratch_shapes=[pltpu.VMEM((tm, D_MODEL), BF16), pltpu.VMEM((tm, D_MODEL), F32)],
        compiler_params=_compiler_params(("parallel", "arbitrary"), blocks),
        name="mlp_final_norm",
    )(h2d, norm_w.reshape(1, D_MODEL), w_up, w_down, final_w.reshape(1, D_MODEL))


def _encode(x, p):
    b, seq, d = x.shape
    x2d = x.reshape(b * seq, d)
    z2d = _inproj(x2d, p["norm_mix_w"], p["w_in"])
    z3 = z2d.reshape(b, seq, IN_COLS)
    a = _neighbourhood_attention(z3, p["na_bias"])
    r = _retention(z3, p["rope"], p["ret_tables"])
    m2d = _branch_mix(a.reshape(b * seq, NA_WIDTH), r.reshape(b * seq, RET_V_WIDTH), z2d,
                      p["w_na_out"], p["w_ret_out"])
    h2d = _outproj(x2d, m2d, p["w_out"])
    y2d = _mlp(h2d, p["norm_mlp_w"], p["w_mlp_up"], p["w_mlp_down"], p["norm_final_w"])
    return y2d.reshape(b, seq, d)


def kernel(x_prompt, x_sample, norm_mix_w, w_in, na_rpb, ret_decay_fwd, ret_decay_bwd, w_na_out, w_ret_out,
           w_out, norm_mlp_w, w_mlp_up, w_mlp_down, norm_final_w):
    depth = w_in.shape[0]
    seq = x_prompt.shape[1]
    assert x_sample.shape[1] == seq and seq % RET_TB == 0 and seq % NA_TQ == 0
    assert depth == 1, "the final norm is fused into the last layer's MLP kernel"
    rope = _rope_tables(seq)
    l = 0
    params = {
        "norm_mix_w": norm_mix_w[l].astype(F32),
        "w_in": w_in[l].astype(BF16),
        "na_bias": _na_bias_table(na_rpb[l], seq // GRID_W),
        "rope": rope,
        "ret_tables": _retention_tables(ret_decay_fwd[l], ret_decay_bwd[l]),
        "w_na_out": w_na_out[l].astype(BF16),
        "w_ret_out": w_ret_out[l].astype(BF16),
        "w_out": w_out[l].astype(BF16),
        "norm_mlp_w": norm_mlp_w[l].astype(F32),
        "w_mlp_up": w_mlp_up[l].astype(BF16),
        "w_mlp_down": w_mlp_down[l].astype(BF16),
        "norm_final_w": norm_final_w.astype(F32),
    }
    return _encode(x_prompt, params), _encode(x_sample, params)
```

```python
import functools

import jax
import jax.numpy as jnp
import numpy as np
from jax import lax
from jax.experimental import pallas as pl
from jax.experimental.pallas import tpu as pltpu

F32 = jnp.float32
BF16 = jnp.bfloat16

D_MODEL = 2048
GRID_W = 64
NA_HEADS = 8
NA_HEAD_DIM = 128
NA_WIDTH = NA_HEADS * NA_HEAD_DIM
NA_WIN_ROWS = 8
NA_WIN_COLS = 16
RET_HEADS = 8
RET_QK_DIM = 128
RET_V_DIM = 256
RET_QK_WIDTH = RET_HEADS * RET_QK_DIM
RET_V_WIDTH = RET_HEADS * RET_V_DIM
ROPE_BASE = 10000.0
D_FF = 4 * D_MODEL
EPS = 1e-6

OFF_NA_Q = 0
OFF_NA_K = OFF_NA_Q + NA_WIDTH
OFF_NA_V = OFF_NA_K + NA_WIDTH
OFF_RET_Q = OFF_NA_V + NA_WIDTH
OFF_RET_K = OFF_RET_Q + RET_QK_WIDTH
OFF_RET_V = OFF_RET_K + RET_QK_WIDTH
OFF_RET_G = OFF_RET_V + RET_V_WIDTH
OFF_GATE_A = OFF_RET_G + RET_V_WIDTH
OFF_GATE_B = OFF_GATE_A + D_MODEL
IN_COLS = OFF_GATE_B + D_MODEL

MASK_VALUE = -1e30
V7X_VMEM_BYTES = 64 * 1024 * 1024


def _compiler_params(semantics, block_bytes):
    limit = min(2 * block_bytes + (8 << 20), V7X_VMEM_BYTES - (4 << 20))
    return pltpu.CompilerParams(dimension_semantics=semantics, vmem_limit_bytes=int(limit))


IN_TM = 1024
IN_TN = 1024


def _inproj_kernel(x_ref, nw_ref, w_ref, z_ref, xn_ref):
    @pl.when(pl.program_id(1) == 0)
    def _():
        x = x_ref[...]
        ms = jnp.mean(x * x, axis=-1, keepdims=True)
        xn_ref[...] = (x * lax.rsqrt(ms + EPS) * nw_ref[...]).astype(BF16)

    z_ref[...] = jnp.dot(xn_ref[...], w_ref[...], preferred_element_type=F32).astype(z_ref.dtype)


def _inproj(x2d, norm_w, w_in):
    m = x2d.shape[0]
    tm, tn = IN_TM, IN_TN
    blocks = tm * D_MODEL * 4 + D_MODEL * tn * 2 + tm * tn * 2 + tm * D_MODEL * 2
    return pl.pallas_call(
        _inproj_kernel,
        out_shape=jax.ShapeDtypeStruct((m, IN_COLS), BF16),
        grid=(m // tm, IN_COLS // tn),
        in_specs=[
            pl.BlockSpec((tm, D_MODEL), lambda i, j: (i, 0)),
            pl.BlockSpec((1, D_MODEL), lambda i, j: (0, 0)),
            pl.BlockSpec((D_MODEL, tn), lambda i, j: (0, j)),
        ],
        out_specs=pl.BlockSpec((tm, tn), lambda i, j: (i, j)),
        scratch_shapes=[pltpu.VMEM((tm, D_MODEL), BF16)],
        compiler_params=_compiler_params(("parallel", "arbitrary"), blocks),
        name="norm_inproj",
    )(x2d, norm_w.reshape(1, D_MODEL), w_in)


NA_SUB = 2
NA_Q_ROWS = 8
NA_K_ROWS = 16
NA_TQ = NA_Q_ROWS * GRID_W
NA_TK = NA_K_ROWS * GRID_W
NA_CQ = 16
NA_CK = 32
NA_N_CHUNKS = GRID_W // NA_CQ
NA_CHUNK_Q = NA_Q_ROWS * NA_CQ
NA_CHUNK_K = NA_K_ROWS * NA_CK
SUBLANES = 8


def _na_key_col_starts():
    starts = []
    for cq in range(NA_N_CHUNKS):
        c = np.arange(cq * NA_CQ, (cq + 1) * NA_CQ)
        c0 = np.clip(c - NA_WIN_COLS // 2, 0, GRID_W - NA_WIN_COLS)
        start = min((int(c0.min()) // SUBLANES) * SUBLANES, GRID_W - NA_CK)
        assert start <= c0.min() and c0.max() + NA_WIN_COLS <= start + NA_CK
        starts.append(start)
    return tuple(starts)


NA_KEY_COL_START = _na_key_col_starts()


NA_INVALID_SLOT = 2 * NA_WIN_ROWS - 1
NA_LANES = 128
NA_LANE_REP = NA_LANES // NA_CK


def _na_row_slots(rows):
    ri = np.arange(NA_Q_ROWS)
    kri = np.arange(NA_K_ROWS)
    half = NA_WIN_ROWS // 2
    n_blocks = rows // NA_Q_ROWS
    row_idx = []
    for kind in range(3):
        q_base = {0: 0, 1: NA_Q_ROWS, 2: (n_blocks - 1) * NA_Q_ROWS}[kind]
        k_base = int(np.clip(q_base - half, 0, rows - NA_K_ROWS))
        r = q_base + ri
        r0 = np.clip(r - half, 0, rows - NA_WIN_ROWS)
        kr = k_base + kri
        valid = (kr[None, :] >= r0[:, None]) & (kr[None, :] < r0[:, None] + NA_WIN_ROWS)
        idx = kr[None, :] - r[:, None] + NA_WIN_ROWS - 1
        row_idx.append(np.where(valid, idx, NA_INVALID_SLOT))
    return np.stack(row_idx)


def _na_column_table(rpb):
    cl = np.arange(NA_CQ)
    kcl = np.arange(NA_CK)
    c = (np.arange(NA_N_CHUNKS) * NA_CQ)[:, None, None] + cl[None, :, None]
    kc = np.asarray(NA_KEY_COL_START)[:, None, None] + kcl[None, None, :]
    c0 = np.clip(c - NA_WIN_COLS // 2, 0, GRID_W - NA_WIN_COLS)
    col_valid = (kc >= c0) & (kc < c0 + NA_WIN_COLS)
    col_idx = np.clip(kc - c + NA_WIN_COLS - 1, 0, 2 * NA_WIN_COLS - 2)
    tc = jnp.where(col_valid[None, None], rpb.astype(F32)[:, :, col_idx], MASK_VALUE)
    tc = jnp.concatenate([tc, jnp.full((NA_HEADS, 1) + tc.shape[2:], MASK_VALUE, F32)], axis=1)
    tc = jnp.transpose(tc, (0, 2, 1, 3, 4))
    return jnp.tile(tc, (1, 1, 1, 1, NA_LANE_REP))


def _na_build_bias(tc_ref, bias_ref, row_slots):
    lane = lax.broadcasted_iota(jnp.int32, (NA_CQ, NA_LANES), 1)
    for kind in range(3):
        for cq in range(NA_N_CHUNKS):
            for ri in range(NA_Q_ROWS):
                for grp in range(NA_K_ROWS // NA_LANE_REP):
                    slots = [int(row_slots[kind, ri, grp * NA_LANE_REP + t]) for t in range(NA_LANE_REP)]
                    tile = tc_ref[cq, slots[-1]]
                    for t in range(NA_LANE_REP - 2, -1, -1):
                        if slots[t] != slots[t + 1]:
                            tile = jnp.where(lane < (t + 1) * NA_CK, tc_ref[cq, slots[t]], tile)
                    bias_ref[kind, cq, ri * NA_CQ:(ri + 1) * NA_CQ, grp * NA_LANES:(grp + 1) * NA_LANES] = tile


def _na_kernel(q_ref, k_ref, v_ref, tc_ref, o_ref, bias_ref, *, seq, row_slots):
    hd = NA_HEAD_DIM
    half_rows = (NA_WIN_ROWS // 2) * GRID_W
    n_blocks = seq // NA_TQ

    @pl.when((pl.program_id(1) == 0) & (pl.program_id(2) == 0))
    def _():
        _na_build_bias(tc_ref, bias_ref, row_slots)

    pairs = [(sb, cq) for sb in range(NA_SUB) for cq in range(NA_N_CHUNKS)]
    qs, ks, vs, kinds = [], [], [], []
    for sb in range(NA_SUB):
        blk = pl.program_id(2) * NA_SUB + sb
        start = pl.multiple_of(jnp.clip(blk * NA_TQ - half_rows, 0, seq - NA_TK), half_rows)
        qs.append(q_ref[sb * NA_TQ:(sb + 1) * NA_TQ, :].astype(F32).reshape(NA_Q_ROWS, GRID_W, hd))
        ks.append(k_ref[pl.ds(start, NA_TK), :].astype(F32).reshape(NA_K_ROWS, GRID_W, hd))
        vs.append(v_ref[pl.ds(start, NA_TK), :].astype(F32).reshape(NA_K_ROWS, GRID_W, hd))
        kinds.append(jnp.where(blk == 0, 0, jnp.where(blk == n_blocks - 1, 2, 1)))
    scores = []
    for sb, cq in pairs:
        k0 = NA_KEY_COL_START[cq]
        qc = qs[sb][:, cq * NA_CQ:(cq + 1) * NA_CQ, :].reshape(NA_CHUNK_Q, hd).astype(BF16)
        kc = ks[sb][:, k0:k0 + NA_CK, :].reshape(NA_CHUNK_K, hd).astype(BF16)
        scores.append(lax.dot_general(qc, kc, (((1,), (1,)), ((), ())), preferred_element_type=F32))
    probs, denoms = [], []
    for (sb, cq), s in zip(pairs, scores):
        s = s * (hd ** -0.5) + bias_ref[kinds[sb], cq]
        p = jnp.exp(s - jnp.max(s, axis=-1, keepdims=True))
        denoms.append(jnp.sum(p, axis=-1, keepdims=True))
        probs.append(p.astype(BF16))
    outs = []
    for (sb, cq), p, l in zip(pairs, probs, denoms):
        k0 = NA_KEY_COL_START[cq]
        vc = vs[sb][:, k0:k0 + NA_CK, :].reshape(NA_CHUNK_K, hd).astype(BF16)
        o = jnp.dot(p, vc, preferred_element_type=F32) / l
        outs.append(o.reshape(NA_Q_ROWS, NA_CQ, hd))
    for sb in range(NA_SUB):
        o = jnp.concatenate(outs[sb * NA_N_CHUNKS:(sb + 1) * NA_N_CHUNKS], axis=1)
        o_ref[sb * NA_TQ:(sb + 1) * NA_TQ, :] = o.reshape(NA_TQ, hd).astype(o_ref.dtype)


def _neighbourhood_attention(z3, col_table):
    b, seq, _ = z3.shape
    tq = NA_SUB * NA_TQ
    hd = NA_HEAD_DIM
    row_slots = _na_row_slots(seq // GRID_W)
    table_bytes = NA_N_CHUNKS * (NA_INVALID_SLOT + 1) * NA_CQ * NA_LANES * 4
    bias_bytes = 3 * NA_N_CHUNKS * NA_CHUNK_Q * NA_CHUNK_K * 4
    blocks = tq * hd * 2 * 2 + 2 * seq * hd * 2 + table_bytes
    temporaries = NA_SUB * (3 * NA_TK * hd * 4 + 3 * NA_N_CHUNKS * NA_CHUNK_Q * NA_CHUNK_K * 4)
    params = pltpu.CompilerParams(
        dimension_semantics=("arbitrary", "arbitrary", "arbitrary"),
        vmem_limit_bytes=int(2 * blocks + bias_bytes + temporaries + (8 << 20)))
    return pl.pallas_call(
        functools.partial(_na_kernel, seq=seq, row_slots=row_slots),
        out_shape=jax.ShapeDtypeStruct((b, seq, NA_WIDTH), BF16),
        grid=(NA_HEADS, b, seq // tq),
        in_specs=[
            pl.BlockSpec((None, tq, hd), lambda h, bi, s: (bi, s, OFF_NA_Q // hd + h)),
            pl.BlockSpec((None, seq, hd), lambda h, bi, s: (bi, 0, OFF_NA_K // hd + h)),
            pl.BlockSpec((None, seq, hd), lambda h, bi, s: (bi, 0, OFF_NA_V // hd + h)),
            pl.BlockSpec((None, NA_N_CHUNKS, NA_INVALID_SLOT + 1, NA_CQ, NA_LANES),
                         lambda h, bi, s: (h, 0, 0, 0, 0)),
        ],
        out_specs=pl.BlockSpec((None, tq, hd), lambda h, bi, s: (bi, s, h)),
        scratch_shapes=[pltpu.VMEM((3, NA_N_CHUNKS, NA_CHUNK_Q, NA_CHUNK_K), F32)],
        compiler_params=params,
        name="neighbourhood_attention",
    )(z3, z3, z3, col_table)


RET_CHUNK = 256
RET_TB = 2048
RET_NC = RET_TB // RET_CHUNK


def _rope_tables(seq):
    half = RET_QK_DIM // 2
    inv_freq = ROPE_BASE ** (-jnp.arange(half, dtype=F32) / half)
    ang = jnp.arange(seq, dtype=F32)[:, None] * inv_freq[None, :]
    cos, sin = jnp.cos(ang), jnp.sin(ang)
    return jnp.concatenate([cos, cos], axis=-1), jnp.concatenate([-sin, sin], axis=-1)


def _decay_tables(decay, backward):
    c = RET_CHUNK
    lg = jax.nn.log_sigmoid(decay.astype(F32))
    pos = jnp.arange(c, dtype=F32)
    diff = pos[:, None] - pos[None, :]
    lg3 = lg[:, None, None]
    if backward:
        decay_in = jnp.exp(jnp.where(diff < 0, -diff * lg3, -jnp.inf))
        q_dec = jnp.exp((c - pos)[None, :] * lg[:, None])
        k_dec = jnp.exp(pos[None, :] * lg[:, None])
    else:
        decay_in = jnp.exp(jnp.where(diff >= 0, diff * lg3, -jnp.inf))
        q_dec = jnp.exp((pos + 1.0)[None, :] * lg[:, None])
        k_dec = jnp.exp((c - 1.0 - pos)[None, :] * lg[:, None])
    chunk_dec = jnp.exp(c * lg)
    return decay_in, q_dec, k_dec, chunk_dec


def _retention_tables(decay_fwd, decay_bwd):
    c, dk = RET_CHUNK, RET_QK_DIM
    din_f, qd_f, kd_f, cd_f = _decay_tables(decay_fwd, False)
    din_b, qd_b, kd_b, cd_b = _decay_tables(decay_bwd, True)
    qd2 = jnp.concatenate([jnp.broadcast_to(qd_f[:, :, None], (RET_HEADS, c, dk)),
                           jnp.broadcast_to(qd_b[:, :, None], (RET_HEADS, c, dk))], axis=2)
    kdt2 = jnp.concatenate([jnp.broadcast_to(kd_f[:, None, :], (RET_HEADS, dk, c)),
                            jnp.broadcast_to(kd_b[:, None, :], (RET_HEADS, dk, c))], axis=1)
    return din_f + din_b, qd2, kdt2, jnp.stack([cd_f, cd_b])


def _retention_kernel(cd_ref, q_ref, k_ref, v_ref, cos_ref, sin_ref, din_ref, qd_ref, kdt_ref, g_ref,
                      o_ref, statef_ref, stateb_ref, kt_ref, kvf_ref, sb_ref, *, n_blk):
    h = pl.program_id(1)
    pass_id = pl.program_id(2)
    j = pl.program_id(3)
    dk, half = RET_QK_DIM, RET_QK_DIM // 2
    chunks = range(RET_NC)
    slices = [slice(c * RET_CHUNK, (c + 1) * RET_CHUNK) for c in chunks]

    @pl.when(j == 0)
    def _():
        statef_ref[...] = jnp.zeros_like(statef_ref)
        stateb_ref[...] = jnp.zeros_like(stateb_ref)

    def rotate(ref):
        x = ref[...].astype(F32)
        return x * cos_ref[...] + pltpu.roll(x, half, 1) * sin_ref[...]

    @pl.when(pass_id == 0)
    def _():
        blk = n_blk - 1 - j
        kt = (rotate(k_ref) * (dk ** -0.5)).T
        kt_ref[blk] = kt.astype(BF16)
        kdt2 = kdt_ref[...]
        kv = [jnp.dot((jnp.concatenate([kt[:, sl], kt[:, sl]], axis=0) * kdt2).astype(BF16), v_ref[sl, :],
                      preferred_element_type=F32) for sl in slices]
        cd_b = cd_ref[1, h]
        state = stateb_ref[...]
        for c in reversed(chunks):
            sb_ref[blk * RET_NC + c] = state.astype(BF16)
            kvf_ref[blk * RET_NC + c] = kv[c][:dk]
            state = cd_b * state + kv[c][dk:]
        stateb_ref[...] = state

    @pl.when(pass_id == 1)
    def _():
        blk = j
        q = rotate(q_ref)
        kt = kt_ref[blk]
        din, qd2 = din_ref[...], qd_ref[...]
        scores = [jnp.dot(q[sl].astype(BF16), kt[:, sl], preferred_element_type=F32) for sl in slices]
        intra = [jnp.dot((s * din).astype(BF16), v_ref[sl, :], preferred_element_type=F32)
                 for s, sl in zip(scores, slices)]
        cd_f = cd_ref[0, h]
        states = [statef_ref[...]]
        for c in chunks:
            states.append(cd_f * states[-1] + kvf_ref[blk * RET_NC + c])
        statef_ref[...] = states[-1]
        for c, sl in zip(chunks, slices):
            lhs = (jnp.concatenate([q[sl], q[sl]], axis=1) * qd2).astype(BF16)
            rhs = jnp.concatenate([states[c].astype(BF16), sb_ref[blk * RET_NC + c]], axis=0)
            o = intra[c] + jnp.dot(lhs, rhs, preferred_element_type=F32)
            o = o * lax.rsqrt(jnp.mean(o * o, axis=-1, keepdims=True) + EPS)
            g = g_ref[sl, :].astype(F32)
            o_ref[sl, :] = (o * (g * jax.nn.sigmoid(g))).astype(o_ref.dtype)


def _retention(z3, rope, tables):
    b, seq, _ = z3.shape
    n_blk = seq // RET_TB
    dk, dv, c = RET_QK_DIM, RET_V_DIM, RET_CHUNK
    cos2, sin2 = rope
    din, qd, kdt, cd = tables

    def tok(p, j):
        return jnp.where(p == 0, n_blk - 1 - j, j)

    def pass1_blk(p, j):
        return jnp.where(p == 0, 0, j)

    def pass0_blk(p, j):
        return jnp.where(p == 0, n_blk - 1 - j, 0)

    n_chunks = seq // c
    blocks = RET_TB * (2 * dk * 2 + dv * 2 + 2 * dk * 4 + 2 * dv * 2) + 3 * c * c * 4
    scratch_bytes = 2 * dk * dv * 4 + seq * dk * 2 + n_chunks * dk * dv * (4 + 2)
    temporaries = 6 * RET_TB * dk * 4 + 16 * c * c * 4
    params = pltpu.CompilerParams(
        dimension_semantics=("parallel", "parallel", "arbitrary", "arbitrary"),
        vmem_limit_bytes=int(2 * blocks + scratch_bytes + temporaries + (8 << 20)))
    return pl.pallas_call(
        functools.partial(_retention_kernel, n_blk=n_blk),
        out_shape=jax.ShapeDtypeStruct((b, seq, RET_V_WIDTH), BF16),
        grid=(b, RET_HEADS, 2, n_blk),
        in_specs=[
            pl.BlockSpec(memory_space=pltpu.SMEM),
            pl.BlockSpec((None, RET_TB, dk), lambda bi, h, p, j: (bi, pass1_blk(p, j), OFF_RET_Q // dk + h)),
            pl.BlockSpec((None, RET_TB, dk), lambda bi, h, p, j: (bi, pass0_blk(p, j), OFF_RET_K // dk + h)),
            pl.BlockSpec((None, RET_TB, dv), lambda bi, h, p, j: (bi, tok(p, j), OFF_RET_V // dv + h)),
            pl.BlockSpec((RET_TB, dk), lambda bi, h, p, j: (tok(p, j), 0)),
            pl.BlockSpec((RET_TB, dk), lambda bi, h, p, j: (tok(p, j), 0)),
            pl.BlockSpec((None, c, c), lambda bi, h, p, j: (h, 0, 0)),
            pl.BlockSpec((None, c, 2 * dk), lambda bi, h, p, j: (h, 0, 0)),
            pl.BlockSpec((None, 2 * dk, c), lambda bi, h, p, j: (h, 0, 0)),
            pl.BlockSpec((None, RET_TB, dv), lambda bi, h, p, j: (bi, pass1_blk(p, j), OFF_RET_G // dv + h)),
        ],
        out_specs=pl.BlockSpec((None, RET_TB, dv), lambda bi, h, p, j: (bi, pass1_blk(p, j), h)),
        scratch_shapes=[
            pltpu.VMEM((dk, dv), F32),
            pltpu.VMEM((dk, dv), F32),
            pltpu.VMEM((n_blk, dk, RET_TB), BF16),
            pltpu.VMEM((n_chunks, dk, dv), F32),
            pltpu.VMEM((n_chunks, dk, dv), BF16),
        ],
        compiler_params=params,
        name="retention",
    )(cd, z3, z3, z3, cos2, sin2, din, qd, kdt, z3)


MIX_TM = 1024
MIX_TN = 1024


def _mix_kernel(a_ref, r_ref, wna_ref, wret_ref, ga_ref, gb_ref, m_ref):
    a = jnp.dot(a_ref[...], wna_ref[...], preferred_element_type=F32)
    r = jnp.dot(r_ref[...], wret_ref[...], preferred_element_type=F32)
    ga = jax.nn.sigmoid(ga_ref[...].astype(F32))
    gb = jax.nn.sigmoid(gb_ref[...].astype(F32))
    m_ref[...] = (ga * a + gb * r).astype(m_ref.dtype)


def _branch_mix(a2d, r2d, z2d, w_na_out, w_ret_out):
    m = a2d.shape[0]
    tm, tn = MIX_TM, MIX_TN
    blocks = tm * NA_WIDTH * 2 + tm * RET_V_WIDTH * 2 + (NA_WIDTH + RET_V_WIDTH) * tn * 2 + 3 * tm * tn * 2
    blocks += 4 * tm * tn * 4
    return pl.pallas_call(
        _mix_kernel,
        out_shape=jax.ShapeDtypeStruct((m, D_MODEL), BF16),
        grid=(m // tm, D_MODEL // tn),
        in_specs=[
            pl.BlockSpec((tm, NA_WIDTH), lambda i, j: (i, 0)),
            pl.BlockSpec((tm, RET_V_WIDTH), lambda i, j: (i, 0)),
            pl.BlockSpec((NA_WIDTH, tn), lambda i, j: (0, j)),
            pl.BlockSpec((RET_V_WIDTH, tn), lambda i, j: (0, j)),
            pl.BlockSpec((tm, tn), lambda i, j: (i, OFF_GATE_A // tn + j)),
            pl.BlockSpec((tm, tn), lambda i, j: (i, OFF_GATE_B // tn + j)),
        ],
        out_specs=pl.BlockSpec((tm, tn), lambda i, j: (i, j)),
        compiler_params=_compiler_params(("parallel", "arbitrary"), blocks),
        name="branch_mix",
    )(a2d, r2d, w_na_out, w_ret_out, z2d, z2d)


OUT_TM = 1024
OUT_TN = 1024


def _outproj_kernel(x_ref, m_ref, w_ref, h_ref):
    h_ref[...] = x_ref[...] + jnp.dot(m_ref[...], w_ref[...], preferred_element_type=F32)


def _outproj(x2d, m2d, w_out):
    m = x2d.shape[0]
    tm, tn = OUT_TM, OUT_TN
    blocks = 2 * tm * tn * 4 + tm * D_MODEL * 2 + D_MODEL * tn * 2 + tm * tn * 4
    return pl.pallas_call(
        _outproj_kernel,
        out_shape=jax.ShapeDtypeStruct((m, D_MODEL), F32),
        grid=(m // tm, D_MODEL // tn),
        in_specs=[
            pl.BlockSpec((tm, tn), lambda i, j: (i, j)),
            pl.BlockSpec((tm, D_MODEL), lambda i, j: (i, 0)),
            pl.BlockSpec((D_MODEL, tn), lambda i, j: (0, j)),
        ],
        out_specs=pl.BlockSpec((tm, tn), lambda i, j: (i, j)),
        compiler_params=_compiler_params(("parallel", "arbitrary"), blocks),
        name="outproj_residual",
    )(x2d, m2d, w_out)


MLP_TM = 512
MLP_TF = 1024


def _mlp_kernel(h_ref, nw_ref, wup_ref, wdown_ref, fw_ref, y_ref, hn_ref, acc_ref):
    f = pl.program_id(1)

    @pl.when(f == 0)
    def _():
        h = h_ref[...]
        ms = jnp.mean(h * h, axis=-1, keepdims=True)
        hn_ref[...] = (h * lax.rsqrt(ms + EPS) * nw_ref[...]).astype(BF16)
        acc_ref[...] = jnp.zeros_like(acc_ref)

    u = jnp.maximum(jnp.dot(hn_ref[...], wup_ref[...], preferred_element_type=F32), 0.0)
    acc_ref[...] += jnp.dot((u * u).astype(BF16), wdown_ref[...], preferred_element_type=F32)

    @pl.when(f == pl.num_programs(1) - 1)
    def _():
        x = h_ref[...] + acc_ref[...]
        ms = jnp.mean(x * x, axis=-1, keepdims=True)
        y_ref[...] = x * lax.rsqrt(ms + EPS) * fw_ref[...]


def _mlp(h2d, norm_w, w_up, w_down, final_w):
    m = h2d.shape[0]
    tm, tf = MLP_TM, MLP_TF
    blocks = 2 * tm * D_MODEL * 4 + 2 * D_MODEL * tf * 2 + tm * D_MODEL * 2 + tm * D_MODEL * 4 + 2 * tm * tf * 4
    return pl.pallas_call(
        _mlp_kernel,
        out_shape=jax.ShapeDtypeStruct((m, D_MODEL), F32),
        grid=(m // tm, D_FF // tf),
        in_specs=[
            pl.BlockSpec((tm, D_MODEL), lambda i, f: (i, 0)),
            pl.BlockSpec((1, D_MODEL), lambda i, f: (0, 0)),
            pl.BlockSpec((D_MODEL, tf), lambda i, f: (0, f)),
            pl.BlockSpec((tf, D_MODEL), lambda i, f: (f, 0)),
            pl.BlockSpec((1, D_MODEL), lambda i, f: (0, 0)),
        ],
        out_specs=pl.BlockSpec((tm, D_MODEL), lambda i, f: (i, 0)),
        scratch_shapes=[pltpu.VMEM((tm, D_MODEL), BF16), pltpu.VMEM((tm, D_MODEL), F32)],
        compiler_params=_compiler_params(("parallel", "arbitrary"), blocks),
        name="mlp_final_norm",
    )(h2d, norm_w.reshape(1, D_MODEL), w_up, w_down, final_w.reshape(1, D_MODEL))


def _encode(x, p):
    b, seq, d = x.shape
    x2d = x.reshape(b * seq, d)
    z2d = _inproj(x2d, p["norm_mix_w"], p["w_in"])
    z3 = z2d.reshape(b, seq, IN_COLS)
    a = _neighbourhood_attention(z3, p["na_bias"])
    r = _retention(z3, p["rope"], p["ret_tables"])
    m2d = _branch_mix(a.reshape(b * seq, NA_WIDTH), r.reshape(b * seq, RET_V_WIDTH), z2d,
                      p["w_na_out"], p["w_ret_out"])
    h2d = _outproj(x2d, m2d, p["w_out"])
    y2d = _mlp(h2d, p["norm_mlp_w"], p["w_mlp_up"], p["w_mlp_down"], p["norm_final_w"])
    return y2d.reshape(b, seq, d)


def kernel(x_prompt, x_sample, norm_mix_w, w_in, na_rpb, ret_decay_fwd, ret_decay_bwd, w_na_out, w_ret_out,
           w_out, norm_mlp_w, w_mlp_up, w_mlp_down, norm_final_w):
    depth = w_in.shape[0]
    seq = x_prompt.shape[1]
    assert x_sample.shape[1] == seq and seq % RET_TB == 0 and seq % (NA_SUB * NA_TQ) == 0
    assert depth == 1, "the final norm is fused into the last layer's MLP kernel"
    rope = _rope_tables(seq)
    l = 0
    params = {
        "norm_mix_w": norm_mix_w[l].astype(F32),
        "w_in": w_in[l].astype(BF16),
        "na_bias": _na_column_table(na_rpb[l]),
        "rope": rope,
        "ret_tables": _retention_tables(ret_decay_fwd[l], ret_decay_bwd[l]),
        "w_na_out": w_na_out[l].astype(BF16),
        "w_ret_out": w_ret_out[l].astype(BF16),
        "w_out": w_out[l].astype(BF16),
        "norm_mlp_w": norm_mlp_w[l].astype(F32),
        "w_mlp_up": w_mlp_up[l].astype(BF16),
        "w_mlp_down": w_mlp_down[l].astype(BF16),
        "norm_final_w": norm_final_w.astype(F32),
    }
    return _encode(x_prompt, params), _encode(x_sample, params)
```

```python
import functools

import jax
import jax.numpy as jnp
import numpy as np
from jax import lax
from jax.experimental import pallas as pl
from jax.experimental.pallas import tpu as pltpu

F32 = jnp.float32
BF16 = jnp.bfloat16

D_MODEL = 2048
GRID_W = 64
NA_HEADS = 8
NA_HEAD_DIM = 128
NA_WIDTH = NA_HEADS * NA_HEAD_DIM
NA_WIN_ROWS = 8
NA_WIN_COLS = 16
RET_HEADS = 8
RET_QK_DIM = 128
RET_V_DIM = 256
RET_QK_WIDTH = RET_HEADS * RET_QK_DIM
RET_V_WIDTH = RET_HEADS * RET_V_DIM
ROPE_BASE = 10000.0
D_FF = 4 * D_MODEL
EPS = 1e-6

OFF_NA_Q = 0
OFF_NA_K = OFF_NA_Q + NA_WIDTH
OFF_NA_V = OFF_NA_K + NA_WIDTH
OFF_RET_Q = OFF_NA_V + NA_WIDTH
OFF_RET_K = OFF_RET_Q + RET_QK_WIDTH
OFF_RET_V = OFF_RET_K + RET_QK_WIDTH
OFF_RET_G = OFF_RET_V + RET_V_WIDTH
OFF_GATE_A = OFF_RET_G + RET_V_WIDTH
OFF_GATE_B = OFF_GATE_A + D_MODEL
IN_COLS = OFF_GATE_B + D_MODEL

MASK_VALUE = -1e30
V7X_VMEM_BYTES = 64 * 1024 * 1024


def _compiler_params(semantics, block_bytes):
    limit = min(2 * block_bytes + (8 << 20), V7X_VMEM_BYTES - (4 << 20))
    return pltpu.CompilerParams(dimension_semantics=semantics, vmem_limit_bytes=int(limit))


IN_TM = 1024
IN_TN = 1024


def _inproj_kernel(x_ref, nw_ref, w_ref, z_ref, xn_ref):
    @pl.when(pl.program_id(1) == 0)
    def _():
        x = x_ref[...]
        ms = jnp.mean(x * x, axis=-1, keepdims=True)
        xn_ref[...] = (x * lax.rsqrt(ms + EPS) * nw_ref[...]).astype(BF16)

    z_ref[...] = jnp.dot(xn_ref[...], w_ref[...], preferred_element_type=F32).astype(z_ref.dtype)


def _inproj(x2d, norm_w, w_in):
    m = x2d.shape[0]
    tm, tn = IN_TM, IN_TN
    blocks = tm * D_MODEL * 4 + D_MODEL * tn * 2 + tm * tn * 2 + tm * D_MODEL * 2
    return pl.pallas_call(
        _inproj_kernel,
        out_shape=jax.ShapeDtypeStruct((m, IN_COLS), BF16),
        grid=(m // tm, IN_COLS // tn),
        in_specs=[
            pl.BlockSpec((tm, D_MODEL), lambda i, j: (i, 0)),
            pl.BlockSpec((1, D_MODEL), lambda i, j: (0, 0)),
            pl.BlockSpec((D_MODEL, tn), lambda i, j: (0, j)),
        ],
        out_specs=pl.BlockSpec((tm, tn), lambda i, j: (i, j)),
        scratch_shapes=[pltpu.VMEM((tm, D_MODEL), BF16)],
        compiler_params=_compiler_params(("parallel", "arbitrary"), blocks),
        name="norm_inproj",
    )(x2d, norm_w.reshape(1, D_MODEL), w_in)


NA_SUB = 4
NA_Q_ROWS = 8
LOG2_E = 1.4426950408889634
NA_Q_SCALE = NA_HEAD_DIM ** -0.5 * LOG2_E
NA_K_ROWS = 16
NA_TQ = NA_Q_ROWS * GRID_W
NA_TK = NA_K_ROWS * GRID_W
NA_CQ = 16
NA_CK = 32
NA_N_CHUNKS = GRID_W // NA_CQ
NA_CHUNK_Q = NA_Q_ROWS * NA_CQ
NA_CHUNK_K = NA_K_ROWS * NA_CK
SUBLANES = 8


def _na_key_col_starts():
    starts = []
    for cq in range(NA_N_CHUNKS):
        c = np.arange(cq * NA_CQ, (cq + 1) * NA_CQ)
        c0 = np.clip(c - NA_WIN_COLS // 2, 0, GRID_W - NA_WIN_COLS)
        start = min((int(c0.min()) // SUBLANES) * SUBLANES, GRID_W - NA_CK)
        assert start <= c0.min() and c0.max() + NA_WIN_COLS <= start + NA_CK
        starts.append(start)
    return tuple(starts)


NA_KEY_COL_START = _na_key_col_starts()


NA_INVALID_SLOT = 2 * NA_WIN_ROWS - 1
NA_LANES = 128
NA_LANE_REP = NA_LANES // NA_CK


def _na_row_slots(rows):
    ri = np.arange(NA_Q_ROWS)
    kri = np.arange(NA_K_ROWS)
    half = NA_WIN_ROWS // 2
    n_blocks = rows // NA_Q_ROWS
    row_idx = []
    for kind in range(3):
        q_base = {0: 0, 1: NA_Q_ROWS, 2: (n_blocks - 1) * NA_Q_ROWS}[kind]
        k_base = int(np.clip(q_base - half, 0, rows - NA_K_ROWS))
        r = q_base + ri
        r0 = np.clip(r - half, 0, rows - NA_WIN_ROWS)
        kr = k_base + kri
        valid = (kr[None, :] >= r0[:, None]) & (kr[None, :] < r0[:, None] + NA_WIN_ROWS)
        idx = kr[None, :] - r[:, None] + NA_WIN_ROWS - 1
        row_idx.append(np.where(valid, idx, NA_INVALID_SLOT))
    return np.stack(row_idx)


def _na_column_table(rpb):
    cl = np.arange(NA_CQ)
    kcl = np.arange(NA_CK)
    c = (np.arange(NA_N_CHUNKS) * NA_CQ)[:, None, None] + cl[None, :, None]
    kc = np.asarray(NA_KEY_COL_START)[:, None, None] + kcl[None, None, :]
    c0 = np.clip(c - NA_WIN_COLS // 2, 0, GRID_W - NA_WIN_COLS)
    col_valid = (kc >= c0) & (kc < c0 + NA_WIN_COLS)
    col_idx = np.clip(kc - c + NA_WIN_COLS - 1, 0, 2 * NA_WIN_COLS - 2)
    bias = rpb.astype(F32)[:, :, col_idx] * LOG2_E
    tc = jnp.where(col_valid[None, None], bias, MASK_VALUE)
    tc = jnp.concatenate([tc, jnp.full((NA_HEADS, 1) + tc.shape[2:], MASK_VALUE, F32)], axis=1)
    tc = jnp.transpose(tc, (0, 2, 1, 3, 4))
    return jnp.tile(tc, (1, 1, 1, 1, NA_LANE_REP))


def _na_build_bias(tc_ref, bias_ref, row_slots):
    lane = lax.broadcasted_iota(jnp.int32, (NA_CQ, NA_LANES), 1)
    for kind in range(3):
        for cq in range(NA_N_CHUNKS):
            for ri in range(NA_Q_ROWS):
                for grp in range(NA_K_ROWS // NA_LANE_REP):
                    slots = [int(row_slots[kind, ri, grp * NA_LANE_REP + t]) for t in range(NA_LANE_REP)]
                    tile = tc_ref[cq, slots[-1]]
                    for t in range(NA_LANE_REP - 2, -1, -1):
                        if slots[t] != slots[t + 1]:
                            tile = jnp.where(lane < (t + 1) * NA_CK, tc_ref[cq, slots[t]], tile)
                    bias_ref[kind, cq, ri * NA_CQ:(ri + 1) * NA_CQ, grp * NA_LANES:(grp + 1) * NA_LANES] = tile


def _na_kernel(q_ref, k_ref, v_ref, tc_ref, o_ref, bias_ref, *, seq, row_slots):
    hd = NA_HEAD_DIM
    half_rows = (NA_WIN_ROWS // 2) * GRID_W
    n_blocks = seq // NA_TQ

    @pl.when((pl.program_id(1) == 0) & (pl.program_id(2) == 0))
    def _():
        _na_build_bias(tc_ref, bias_ref, row_slots)

    pairs = [(sb, cq) for sb in range(NA_SUB) for cq in range(NA_N_CHUNKS)]
    qs, ks, vs, kinds = [], [], [], []
    for sb in range(NA_SUB):
        blk = pl.program_id(2) * NA_SUB + sb
        start = pl.multiple_of(jnp.clip(blk * NA_TQ - half_rows, 0, seq - NA_TK), half_rows)
        q = q_ref[sb * NA_TQ:(sb + 1) * NA_TQ, :].astype(F32) * NA_Q_SCALE
        qs.append(q.reshape(NA_Q_ROWS, GRID_W, hd))
        ks.append(k_ref[pl.ds(start, NA_TK), :].astype(F32).reshape(NA_K_ROWS, GRID_W, hd))
        vs.append(v_ref[pl.ds(start, NA_TK), :].astype(F32).reshape(NA_K_ROWS, GRID_W, hd))
        kinds.append(jnp.where(blk == 0, 0, jnp.where(blk == n_blocks - 1, 2, 1)))
    scores = []
    for sb, cq in pairs:
        k0 = NA_KEY_COL_START[cq]
        qc = qs[sb][:, cq * NA_CQ:(cq + 1) * NA_CQ, :].reshape(NA_CHUNK_Q, hd).astype(BF16)
        kc = ks[sb][:, k0:k0 + NA_CK, :].reshape(NA_CHUNK_K, hd).astype(BF16)
        scores.append(lax.dot_general(qc, kc, (((1,), (1,)), ((), ())), preferred_element_type=F32))
    probs, denoms = [], []
    for (sb, cq), s in zip(pairs, scores):
        s = s + bias_ref[kinds[sb], cq]
        p = jnp.exp2(s - jnp.max(s, axis=-1, keepdims=True))
        denoms.append(jnp.sum(p, axis=-1, keepdims=True))
        probs.append(p.astype(BF16))
    outs = []
    for (sb, cq), p, l in zip(pairs, probs, denoms):
        k0 = NA_KEY_COL_START[cq]
        vc = vs[sb][:, k0:k0 + NA_CK, :].reshape(NA_CHUNK_K, hd).astype(BF16)
        o = jnp.dot(p, vc, preferred_element_type=F32) / l
        outs.append(o.reshape(NA_Q_ROWS, NA_CQ, hd))
    for sb in range(NA_SUB):
        o = jnp.concatenate(outs[sb * NA_N_CHUNKS:(sb + 1) * NA_N_CHUNKS], axis=1)
        o_ref[sb * NA_TQ:(sb + 1) * NA_TQ, :] = o.reshape(NA_TQ, hd).astype(o_ref.dtype)


def _neighbourhood_attention(z3, col_table):
    b, seq, _ = z3.shape
    tq = NA_SUB * NA_TQ
    hd = NA_HEAD_DIM
    row_slots = _na_row_slots(seq // GRID_W)
    table_bytes = NA_N_CHUNKS * (NA_INVALID_SLOT + 1) * NA_CQ * NA_LANES * 4
    bias_bytes = 3 * NA_N_CHUNKS * NA_CHUNK_Q * NA_CHUNK_K * 4
    blocks = tq * hd * 2 * 2 + 2 * seq * hd * 2 + table_bytes
    temporaries = NA_SUB * (3 * NA_TK * hd * 4 + 3 * NA_N_CHUNKS * NA_CHUNK_Q * NA_CHUNK_K * 4)
    params = pltpu.CompilerParams(
        dimension_semantics=("arbitrary", "arbitrary", "arbitrary"),
        vmem_limit_bytes=int(2 * blocks + bias_bytes + temporaries + (8 << 20)))
    return pl.pallas_call(
        functools.partial(_na_kernel, seq=seq, row_slots=row_slots),
        out_shape=jax.ShapeDtypeStruct((b, seq, NA_WIDTH), BF16),
        grid=(NA_HEADS, b, seq // tq),
        in_specs=[
            pl.BlockSpec((None, tq, hd), lambda h, bi, s: (bi, s, OFF_NA_Q // hd + h)),
            pl.BlockSpec((None, seq, hd), lambda h, bi, s: (bi, 0, OFF_NA_K // hd + h)),
            pl.BlockSpec((None, seq, hd), lambda h, bi, s: (bi, 0, OFF_NA_V // hd + h)),
            pl.BlockSpec((None, NA_N_CHUNKS, NA_INVALID_SLOT + 1, NA_CQ, NA_LANES),
                         lambda h, bi, s: (h, 0, 0, 0, 0)),
        ],
        out_specs=pl.BlockSpec((None, tq, hd), lambda h, bi, s: (bi, s, h)),
        scratch_shapes=[pltpu.VMEM((3, NA_N_CHUNKS, NA_CHUNK_Q, NA_CHUNK_K), F32)],
        compiler_params=params,
        name="neighbourhood_attention",
    )(z3, z3, z3, col_table)


RET_CHUNK = 256
RET_TB = 4096
RET_NC = RET_TB // RET_CHUNK


def _rope_tables(seq):
    half = RET_QK_DIM // 2
    inv_freq = ROPE_BASE ** (-jnp.arange(half, dtype=F32) / half)
    ang = jnp.arange(seq, dtype=F32)[:, None] * inv_freq[None, :]
    cos, sin = jnp.cos(ang), jnp.sin(ang)
    return jnp.concatenate([cos, cos], axis=-1), jnp.concatenate([-sin, sin], axis=-1)


def _decay_tables(decay, backward):
    c = RET_CHUNK
    lg = jax.nn.log_sigmoid(decay.astype(F32))
    pos = jnp.arange(c, dtype=F32)
    diff = pos[:, None] - pos[None, :]
    lg3 = lg[:, None, None]
    if backward:
        decay_in = jnp.exp(jnp.where(diff < 0, -diff * lg3, -jnp.inf))
        q_dec = jnp.exp((c - pos)[None, :] * lg[:, None])
        k_dec = jnp.exp(pos[None, :] * lg[:, None])
    else:
        decay_in = jnp.exp(jnp.where(diff >= 0, diff * lg3, -jnp.inf))
        q_dec = jnp.exp((pos + 1.0)[None, :] * lg[:, None])
        k_dec = jnp.exp((c - 1.0 - pos)[None, :] * lg[:, None])
    chunk_dec = jnp.exp(c * lg)
    return decay_in, q_dec, k_dec, chunk_dec


def _retention_tables(decay_fwd, decay_bwd):
    c, dk = RET_CHUNK, RET_QK_DIM
    din_f, qd_f, kd_f, cd_f = _decay_tables(decay_fwd, False)
    din_b, qd_b, kd_b, cd_b = _decay_tables(decay_bwd, True)
    qd2 = jnp.concatenate([jnp.broadcast_to(qd_f[:, :, None], (RET_HEADS, c, dk)),
                           jnp.broadcast_to(qd_b[:, :, None], (RET_HEADS, c, dk))], axis=2)
    kdt2 = jnp.concatenate([jnp.broadcast_to(kd_f[:, None, :], (RET_HEADS, dk, c)),
                            jnp.broadcast_to(kd_b[:, None, :], (RET_HEADS, dk, c))], axis=1)
    return din_f + din_b, qd2, kdt2, jnp.stack([cd_f, cd_b])


def _retention_kernel(cd_ref, q_ref, k_ref, v_ref, cos_ref, sin_ref, din_ref, qd_ref, kdt_ref, g_ref,
                      o_ref, statef_ref, stateb_ref, kt_ref, kvf_ref, sb_ref, *, n_blk):
    h = pl.program_id(1)
    pass_id = pl.program_id(2)
    j = pl.program_id(3)
    dk, half = RET_QK_DIM, RET_QK_DIM // 2
    chunks = range(RET_NC)
    slices = [slice(c * RET_CHUNK, (c + 1) * RET_CHUNK) for c in chunks]

    @pl.when(j == 0)
    def _():
        statef_ref[...] = jnp.zeros_like(statef_ref)
        stateb_ref[...] = jnp.zeros_like(stateb_ref)

    def rotate(ref):
        x = ref[...].astype(F32)
        return x * cos_ref[...] + pltpu.roll(x, half, 1) * sin_ref[...]

    @pl.when(pass_id == 0)
    def _():
        blk = n_blk - 1 - j
        kt = (rotate(k_ref) * (dk ** -0.5)).T
        kt_ref[blk] = kt.astype(BF16)
        kdt2 = kdt_ref[...]
        kv = [jnp.dot((jnp.concatenate([kt[:, sl], kt[:, sl]], axis=0) * kdt2).astype(BF16), v_ref[sl, :],
                      preferred_element_type=F32) for sl in slices]
        cd_b = cd_ref[1, h]
        state = stateb_ref[...]
        for c in reversed(chunks):
            sb_ref[blk * RET_NC + c] = state.astype(BF16)
            kvf_ref[blk * RET_NC + c] = kv[c][:dk]
            state = cd_b * state + kv[c][dk:]
        stateb_ref[...] = state

    @pl.when(pass_id == 1)
    def _():
        blk = j
        q = rotate(q_ref)
        kt = kt_ref[blk]
        din, qd2 = din_ref[...], qd_ref[...]
        scores = [jnp.dot(q[sl].astype(BF16), kt[:, sl], preferred_element_type=F32) for sl in slices]
        decayed = [(s * din).astype(BF16) for s in scores]
        cd_f = cd_ref[0, h]
        states = [statef_ref[...]]
        for c in chunks:
            states.append(cd_f * states[-1] + kvf_ref[blk * RET_NC + c])
        statef_ref[...] = states[-1]
        for c, sl in zip(chunks, slices):
            lhs = jnp.concatenate(
                [decayed[c], (jnp.concatenate([q[sl], q[sl]], axis=1) * qd2).astype(BF16)], axis=1)
            rhs = jnp.concatenate(
                [v_ref[sl, :], states[c].astype(BF16), sb_ref[blk * RET_NC + c]], axis=0)
            o = jnp.dot(lhs, rhs, preferred_element_type=F32)
            o = o * lax.rsqrt(jnp.mean(o * o, axis=-1, keepdims=True) + EPS)
            g = g_ref[sl, :].astype(F32)
            o_ref[sl, :] = (o * (g * jax.nn.sigmoid(g))).astype(o_ref.dtype)


def _retention(z3, rope, tables):
    b, seq, _ = z3.shape
    n_blk = seq // RET_TB
    dk, dv, c = RET_QK_DIM, RET_V_DIM, RET_CHUNK
    cos2, sin2 = rope
    din, qd, kdt, cd = tables

    def tok(p, j):
        return jnp.where(p == 0, n_blk - 1 - j, j)

    def pass1_blk(p, j):
        return jnp.where(p == 0, 0, j)

    def pass0_blk(p, j):
        return jnp.where(p == 0, n_blk - 1 - j, 0)

    n_chunks = seq // c
    blocks = RET_TB * (2 * dk * 2 + dv * 2 + 2 * dk * 4 + 2 * dv * 2) + 3 * c * c * 4
    scratch_bytes = 2 * dk * dv * 4 + seq * dk * 2 + n_chunks * dk * dv * (4 + 2)
    temporaries = 6 * RET_TB * dk * 4 + 16 * c * c * 4
    params = pltpu.CompilerParams(
        dimension_semantics=("parallel", "parallel", "arbitrary", "arbitrary"),
        vmem_limit_bytes=int(2 * blocks + scratch_bytes + temporaries + (8 << 20)))
    return pl.pallas_call(
        functools.partial(_retention_kernel, n_blk=n_blk),
        out_shape=jax.ShapeDtypeStruct((b, seq, RET_V_WIDTH), BF16),
        grid=(b, RET_HEADS, 2, n_blk),
        in_specs=[
            pl.BlockSpec(memory_space=pltpu.SMEM),
            pl.BlockSpec((None, RET_TB, dk), lambda bi, h, p, j: (bi, pass1_blk(p, j), OFF_RET_Q // dk + h)),
            pl.BlockSpec((None, RET_TB, dk), lambda bi, h, p, j: (bi, pass0_blk(p, j), OFF_RET_K // dk + h)),
            pl.BlockSpec((None, RET_TB, dv), lambda bi, h, p, j: (bi, tok(p, j), OFF_RET_V // dv + h)),
            pl.BlockSpec((RET_TB, dk), lambda bi, h, p, j: (tok(p, j), 0)),
            pl.BlockSpec((RET_TB, dk), lambda bi, h, p, j: (tok(p, j), 0)),
            pl.BlockSpec((None, c, c), lambda bi, h, p, j: (h, 0, 0)),
            pl.BlockSpec((None, c, 2 * dk), lambda bi, h, p, j: (h, 0, 0)),
            pl.BlockSpec((None, 2 * dk, c), lambda bi, h, p, j: (h, 0, 0)),
            pl.BlockSpec((None, RET_TB, dv), lambda bi, h, p, j: (bi, pass1_blk(p, j), OFF_RET_G // dv + h)),
        ],
        out_specs=pl.BlockSpec((None, RET_TB, dv), lambda bi, h, p, j: (bi, pass1_blk(p, j), h)),
        scratch_shapes=[
            pltpu.VMEM((dk, dv), F32),
            pltpu.VMEM((dk, dv), F32),
            pltpu.VMEM((n_blk, dk, RET_TB), BF16),
            pltpu.VMEM((n_chunks, dk, dv), F32),
            pltpu.VMEM((n_chunks, dk, dv), BF16),
        ],
        compiler_params=params,
        name="retention",
    )(cd, z3, z3, z3, cos2, sin2, din, qd, kdt, z3)


MIX_TM = 1024
MIX_TN = 1024


def _mix_kernel(a_ref, r_ref, wna_ref, wret_ref, ga_ref, gb_ref, m_ref):
    a = jnp.dot(a_ref[...], wna_ref[...], preferred_element_type=F32)
    r = jnp.dot(r_ref[...], wret_ref[...], preferred_element_type=F32)
    ga = jax.nn.sigmoid(ga_ref[...].astype(F32))
    gb = jax.nn.sigmoid(gb_ref[...].astype(F32))
    m_ref[...] = (ga * a + gb * r).astype(m_ref.dtype)


def _branch_mix(a2d, r2d, z2d, w_na_out, w_ret_out):
    m = a2d.shape[0]
    tm, tn = MIX_TM, MIX_TN
    blocks = tm * NA_WIDTH * 2 + tm * RET_V_WIDTH * 2 + (NA_WIDTH + RET_V_WIDTH) * tn * 2 + 3 * tm * tn * 2
    blocks += 4 * tm * tn * 4
    return pl.pallas_call(
        _mix_kernel,
        out_shape=jax.ShapeDtypeStruct((m, D_MODEL), BF16),
        grid=(m // tm, D_MODEL // tn),
        in_specs=[
            pl.BlockSpec((tm, NA_WIDTH), lambda i, j: (i, 0)),
            pl.BlockSpec((tm, RET_V_WIDTH), lambda i, j: (i, 0)),
            pl.BlockSpec((NA_WIDTH, tn), lambda i, j: (0, j)),
            pl.BlockSpec((RET_V_WIDTH, tn), lambda i, j: (0, j)),
            pl.BlockSpec((tm, tn), lambda i, j: (i, OFF_GATE_A // tn + j)),
            pl.BlockSpec((tm, tn), lambda i, j: (i, OFF_GATE_B // tn + j)),
        ],
        out_specs=pl.BlockSpec((tm, tn), lambda i, j: (i, j)),
        compiler_params=_compiler_params(("parallel", "arbitrary"), blocks),
        name="branch_mix",
    )(a2d, r2d, w_na_out, w_ret_out, z2d, z2d)


OUT_TM = 1024
OUT_TN = 1024


def _outproj_kernel(x_ref, m_ref, w_ref, h_ref):
    h_ref[...] = x_ref[...] + jnp.dot(m_ref[...], w_ref[...], preferred_element_type=F32)


def _outproj(x2d, m2d, w_out):
    m = x2d.shape[0]
    tm, tn = OUT_TM, OUT_TN
    blocks = 2 * tm * tn * 4 + tm * D_MODEL * 2 + D_MODEL * tn * 2 + tm * tn * 4
    return pl.pallas_call(
        _outproj_kernel,
        out_shape=jax.ShapeDtypeStruct((m, D_MODEL), F32),
        grid=(m // tm, D_MODEL // tn),
        in_specs=[
            pl.BlockSpec((tm, tn), lambda i, j: (i, j)),
            pl.BlockSpec((tm, D_MODEL), lambda i, j: (i, 0)),
            pl.BlockSpec((D_MODEL, tn), lambda i, j: (0, j)),
        ],
        out_specs=pl.BlockSpec((tm, tn), lambda i, j: (i, j)),
        compiler_params=_compiler_params(("parallel", "arbitrary"), blocks),
        name="outproj_residual",
    )(x2d, m2d, w_out)


MLP_TM = 512
MLP_TF = 1024


def _mlp_kernel(h_ref, nw_ref, wup_ref, wdown_ref, fw_ref, y_ref, hn_ref, acc_ref):
    f = pl.program_id(1)

    @pl.when(f == 0)
    def _():
        h = h_ref[...]
        ms = jnp.mean(h * h, axis=-1, keepdims=True)
        hn_ref[...] = (h * lax.rsqrt(ms + EPS) * nw_ref[...]).astype(BF16)
        acc_ref[...] = jnp.zeros_like(acc_ref)

    u = jnp.maximum(jnp.dot(hn_ref[...], wup_ref[...], preferred_element_type=F32), 0.0)
    acc_ref[...] += jnp.dot((u * u).astype(BF16), wdown_ref[...], preferred_element_type=F32)

    @pl.when(f == pl.num_programs(1) - 1)
    def _():
        x = h_ref[...] + acc_ref[...]
        ms = jnp.mean(x * x, axis=-1, keepdims=True)
        y_ref[...] = x * lax.rsqrt(ms + EPS) * fw_ref[...]


def _mlp(h2d, norm_w, w_up, w_down, final_w):
    m = h2d.shape[0]
    tm, tf = MLP_TM, MLP_TF
    blocks = 2 * tm * D_MODEL * 4 + 2 * D_MODEL * tf * 2 + tm * D_MODEL * 2 + tm * D_MODEL * 4 + 2 * tm * tf * 4
    return pl.pallas_call(
        _mlp_kernel,
        out_shape=jax.ShapeDtypeStruct((m, D_MODEL), F32),
        grid=(m // tm, D_FF // tf),
        in_specs=[
            pl.BlockSpec((tm, D_MODEL), lambda i, f: (i, 0)),
            pl.BlockSpec((1, D_MODEL), lambda i, f: (0, 0)),
            pl.BlockSpec((D_MODEL, tf), lambda i, f: (0, f)),
            pl.BlockSpec((tf, D_MODEL), lambda i, f: (f, 0)),
            pl.BlockSpec((1, D_MODEL), lambda i, f: (0, 0)),
        ],
        out_specs=pl.BlockSpec((tm, D_MODEL), lambda i, f: (i, 0)),
        scratch_shapes=[pltpu.VMEM((tm, D_MODEL), BF16), pltpu.VMEM((tm, D_MODEL), F32)],
        compiler_params=_compiler_params(("parallel", "arbitrary"), blocks),
        name="mlp_final_norm",
    )(h2d, norm_w.reshape(1, D_MODEL), w_up, w_down, final_w.reshape(1, D_MODEL))


def _encode(x, p):
    b, seq, d = x.shape
    x2d = x.reshape(b * seq, d)
    z2d = _inproj(x2d, p["norm_mix_w"], p["w_in"])
    z3 = z2d.reshape(b, seq, IN_COLS)
    a = _neighbourhood_attention(z3, p["na_bias"])
    r = _retention(z3, p["rope"], p["ret_tables"])
    m2d = _branch_mix(a.reshape(b * seq, NA_WIDTH), r.reshape(b * seq, RET_V_WIDTH), z2d,
                      p["w_na_out"], p["w_ret_out"])
    h2d = _outproj(x2d, m2d, p["w_out"])
    y2d = _mlp(h2d, p["norm_mlp_w"], p["w_mlp_up"], p["w_mlp_down"], p["norm_final_w"])
    return y2d.reshape(b, seq, d)


def kernel(x_prompt, x_sample, norm_mix_w, w_in, na_rpb, ret_decay_fwd, ret_decay_bwd, w_na_out, w_ret_out,
           w_out, norm_mlp_w, w_mlp_up, w_mlp_down, norm_final_w):
    depth = w_in.shape[0]
    seq = x_prompt.shape[1]
    assert x_sample.shape[1] == seq and seq % RET_TB == 0 and seq % (NA_SUB * NA_TQ) == 0
    assert depth == 1, "the final norm is fused into the last layer's MLP kernel"
    rope = _rope_tables(seq)
    l = 0
    params = {
        "norm_mix_w": norm_mix_w[l].astype(F32),
        "w_in": w_in[l].astype(BF16),
        "na_bias": _na_column_table(na_rpb[l]),
        "rope": rope,
        "ret_tables": _retention_tables(ret_decay_fwd[l], ret_decay_bwd[l]),
        "w_na_out": w_na_out[l].astype(BF16),
        "w_ret_out": w_ret_out[l].astype(BF16),
        "w_out": w_out[l].astype(BF16),
        "norm_mlp_w": norm_mlp_w[l].astype(F32),
        "w_mlp_up": w_mlp_up[l].astype(BF16),
        "w_mlp_down": w_mlp_down[l].astype(BF16),
        "norm_final_w": norm_final_w.astype(F32),
    }
    return _encode(x_prompt, params), _encode(x_sample, params)
```

```python
import functools

import jax
import jax.numpy as jnp
import numpy as np
from jax import lax
from jax.experimental import pallas as pl
from jax.experimental.pallas import tpu as pltpu

F32 = jnp.float32
BF16 = jnp.bfloat16

D_MODEL = 2048
GRID_W = 64
NA_HEADS = 8
NA_HEAD_DIM = 128
NA_WIDTH = NA_HEADS * NA_HEAD_DIM
NA_WIN_ROWS = 8
NA_WIN_COLS = 16
RET_HEADS = 8
RET_QK_DIM = 128
RET_V_DIM = 256
RET_QK_WIDTH = RET_HEADS * RET_QK_DIM
RET_V_WIDTH = RET_HEADS * RET_V_DIM
ROPE_BASE = 10000.0
D_FF = 4 * D_MODEL
EPS = 1e-6

OFF_NA_Q = 0
OFF_NA_K = OFF_NA_Q + NA_WIDTH
OFF_NA_V = OFF_NA_K + NA_WIDTH
OFF_RET_Q = OFF_NA_V + NA_WIDTH
OFF_RET_K = OFF_RET_Q + RET_QK_WIDTH
OFF_RET_V = OFF_RET_K + RET_QK_WIDTH
OFF_RET_G = OFF_RET_V + RET_V_WIDTH
OFF_GATE_A = OFF_RET_G + RET_V_WIDTH
OFF_GATE_B = OFF_GATE_A + D_MODEL
IN_COLS = OFF_GATE_B + D_MODEL

MASK_VALUE = -1e30
V7X_VMEM_BYTES = 64 * 1024 * 1024


def _compiler_params(semantics, block_bytes):
    limit = min(2 * block_bytes + (8 << 20), V7X_VMEM_BYTES - (4 << 20))
    return pltpu.CompilerParams(dimension_semantics=semantics, vmem_limit_bytes=int(limit))


IN_TM = 1024
IN_TN = 1024


def _inproj_kernel(x_ref, nw_ref, w_ref, z_ref, xn_ref):
    @pl.when(pl.program_id(1) == 0)
    def _():
        x = x_ref[...]
        ms = jnp.mean(x * x, axis=-1, keepdims=True)
        xn_ref[...] = (x * lax.rsqrt(ms + EPS) * nw_ref[...]).astype(BF16)

    z_ref[...] = jnp.dot(xn_ref[...], w_ref[...], preferred_element_type=F32).astype(z_ref.dtype)


def _inproj(x2d, norm_w, w_in):
    m = x2d.shape[0]
    tm, tn = IN_TM, IN_TN
    blocks = tm * D_MODEL * 4 + D_MODEL * tn * 2 + tm * tn * 2 + tm * D_MODEL * 2
    return pl.pallas_call(
        _inproj_kernel,
        out_shape=jax.ShapeDtypeStruct((m, IN_COLS), BF16),
        grid=(m // tm, IN_COLS // tn),
        in_specs=[
            pl.BlockSpec((tm, D_MODEL), lambda i, j: (i, 0)),
            pl.BlockSpec((1, D_MODEL), lambda i, j: (0, 0)),
            pl.BlockSpec((D_MODEL, tn), lambda i, j: (0, j)),
        ],
        out_specs=pl.BlockSpec((tm, tn), lambda i, j: (i, j)),
        scratch_shapes=[pltpu.VMEM((tm, D_MODEL), BF16)],
        compiler_params=_compiler_params(("parallel", "arbitrary"), blocks),
        name="norm_inproj",
    )(x2d, norm_w.reshape(1, D_MODEL), w_in)


NA_SUB = 4
NA_Q_ROWS = 8
LOG2_E = 1.4426950408889634
NA_Q_SCALE = NA_HEAD_DIM ** -0.5 * LOG2_E
NA_K_ROWS = 16
NA_TQ = NA_Q_ROWS * GRID_W
NA_TK = NA_K_ROWS * GRID_W
NA_CQ = 16
NA_CK = 32
NA_N_CHUNKS = GRID_W // NA_CQ
NA_CHUNK_Q = NA_Q_ROWS * NA_CQ
NA_CHUNK_K = NA_K_ROWS * NA_CK
SUBLANES = 8


def _na_key_col_starts():
    starts = []
    for cq in range(NA_N_CHUNKS):
        c = np.arange(cq * NA_CQ, (cq + 1) * NA_CQ)
        c0 = np.clip(c - NA_WIN_COLS // 2, 0, GRID_W - NA_WIN_COLS)
        start = min((int(c0.min()) // SUBLANES) * SUBLANES, GRID_W - NA_CK)
        assert start <= c0.min() and c0.max() + NA_WIN_COLS <= start + NA_CK
        starts.append(start)
    return tuple(starts)


NA_KEY_COL_START = _na_key_col_starts()


NA_INVALID_SLOT = 2 * NA_WIN_ROWS - 1
NA_LANES = 128
NA_LANE_REP = NA_LANES // NA_CK


def _na_row_slots(rows):
    ri = np.arange(NA_Q_ROWS)
    kri = np.arange(NA_K_ROWS)
    half = NA_WIN_ROWS // 2
    n_blocks = rows // NA_Q_ROWS
    row_idx = []
    for kind in range(3):
        q_base = {0: 0, 1: NA_Q_ROWS, 2: (n_blocks - 1) * NA_Q_ROWS}[kind]
        k_base = int(np.clip(q_base - half, 0, rows - NA_K_ROWS))
        r = q_base + ri
        r0 = np.clip(r - half, 0, rows - NA_WIN_ROWS)
        kr = k_base + kri
        valid = (kr[None, :] >= r0[:, None]) & (kr[None, :] < r0[:, None] + NA_WIN_ROWS)
        idx = kr[None, :] - r[:, None] + NA_WIN_ROWS - 1
        row_idx.append(np.where(valid, idx, NA_INVALID_SLOT))
    return np.stack(row_idx)


def _na_column_table(rpb):
    cl = np.arange(NA_CQ)
    kcl = np.arange(NA_CK)
    c = (np.arange(NA_N_CHUNKS) * NA_CQ)[:, None, None] + cl[None, :, None]
    kc = np.asarray(NA_KEY_COL_START)[:, None, None] + kcl[None, None, :]
    c0 = np.clip(c - NA_WIN_COLS // 2, 0, GRID_W - NA_WIN_COLS)
    col_valid = (kc >= c0) & (kc < c0 + NA_WIN_COLS)
    col_idx = np.clip(kc - c + NA_WIN_COLS - 1, 0, 2 * NA_WIN_COLS - 2)
    bias = rpb.astype(F32)[:, :, col_idx] * LOG2_E
    tc = jnp.where(col_valid[None, None], bias, MASK_VALUE)
    tc = jnp.concatenate([tc, jnp.full((NA_HEADS, 1) + tc.shape[2:], MASK_VALUE, F32)], axis=1)
    tc = jnp.transpose(tc, (0, 2, 1, 3, 4))
    return jnp.tile(tc, (1, 1, 1, 1, NA_LANE_REP))


def _na_build_bias(tc_ref, bias_ref, row_slots):
    lane = lax.broadcasted_iota(jnp.int32, (NA_CQ, NA_LANES), 1)
    for kind in range(3):
        for cq in range(NA_N_CHUNKS):
            for ri in range(NA_Q_ROWS):
                for grp in range(NA_K_ROWS // NA_LANE_REP):
                    slots = [int(row_slots[kind, ri, grp * NA_LANE_REP + t]) for t in range(NA_LANE_REP)]
                    tile = tc_ref[cq, slots[-1]]
                    for t in range(NA_LANE_REP - 2, -1, -1):
                        if slots[t] != slots[t + 1]:
                            tile = jnp.where(lane < (t + 1) * NA_CK, tc_ref[cq, slots[t]], tile)
                    bias_ref[kind, cq, ri * NA_CQ:(ri + 1) * NA_CQ, grp * NA_LANES:(grp + 1) * NA_LANES] = tile


def _na_kernel(q_ref, k_ref, v_ref, tc_ref, o_ref, bias_ref, *, seq, row_slots):
    hd = NA_HEAD_DIM
    half_rows = (NA_WIN_ROWS // 2) * GRID_W
    n_blocks = seq // NA_TQ

    @pl.when((pl.program_id(1) == 0) & (pl.program_id(2) == 0))
    def _():
        _na_build_bias(tc_ref, bias_ref, row_slots)

    pairs = [(sb, cq) for sb in range(NA_SUB) for cq in range(NA_N_CHUNKS)]
    qs, ks, vs, kinds = [], [], [], []
    for sb in range(NA_SUB):
        blk = pl.program_id(2) * NA_SUB + sb
        start = pl.multiple_of(jnp.clip(blk * NA_TQ - half_rows, 0, seq - NA_TK), half_rows)
        q = q_ref[sb * NA_TQ:(sb + 1) * NA_TQ, :].astype(F32) * NA_Q_SCALE
        qs.append(q.reshape(NA_Q_ROWS, GRID_W, hd))
        ks.append(k_ref[pl.ds(start, NA_TK), :].astype(F32).reshape(NA_K_ROWS, GRID_W, hd))
        vs.append(v_ref[pl.ds(start, NA_TK), :].astype(F32).reshape(NA_K_ROWS, GRID_W, hd))
        kinds.append(jnp.where(blk == 0, 0, jnp.where(blk == n_blocks - 1, 2, 1)))
    scores = []
    for sb, cq in pairs:
        k0 = NA_KEY_COL_START[cq]
        qc = qs[sb][:, cq * NA_CQ:(cq + 1) * NA_CQ, :].reshape(NA_CHUNK_Q, hd).astype(BF16)
        kc = ks[sb][:, k0:k0 + NA_CK, :].reshape(NA_CHUNK_K, hd).astype(BF16)
        scores.append(lax.dot_general(qc, kc, (((1,), (1,)), ((), ())), preferred_element_type=F32))
    probs, denoms = [], []
    for (sb, cq), s in zip(pairs, scores):
        s = s + bias_ref[kinds[sb], cq]
        p = jnp.exp2(s - jnp.max(s, axis=-1, keepdims=True))
        denoms.append(jnp.sum(p, axis=-1, keepdims=True))
        probs.append(p.astype(BF16))
    outs = []
    for (sb, cq), p, l in zip(pairs, probs, denoms):
        k0 = NA_KEY_COL_START[cq]
        vc = vs[sb][:, k0:k0 + NA_CK, :].reshape(NA_CHUNK_K, hd).astype(BF16)
        o = jnp.dot(p, vc, preferred_element_type=F32) / l
        outs.append(o.reshape(NA_Q_ROWS, NA_CQ, hd))
    for sb in range(NA_SUB):
        o = jnp.concatenate(outs[sb * NA_N_CHUNKS:(sb + 1) * NA_N_CHUNKS], axis=1)
        o_ref[sb * NA_TQ:(sb + 1) * NA_TQ, :] = o.reshape(NA_TQ, hd).astype(o_ref.dtype)


def _neighbourhood_attention(z3, col_table):
    b, seq, _ = z3.shape
    tq = NA_SUB * NA_TQ
    hd = NA_HEAD_DIM
    row_slots = _na_row_slots(seq // GRID_W)
    table_bytes = NA_N_CHUNKS * (NA_INVALID_SLOT + 1) * NA_CQ * NA_LANES * 4
    bias_bytes = 3 * NA_N_CHUNKS * NA_CHUNK_Q * NA_CHUNK_K * 4
    blocks = tq * hd * 2 * 2 + 2 * seq * hd * 2 + table_bytes
    temporaries = NA_SUB * (3 * NA_TK * hd * 4 + 3 * NA_N_CHUNKS * NA_CHUNK_Q * NA_CHUNK_K * 4)
    params = pltpu.CompilerParams(
        dimension_semantics=("arbitrary", "arbitrary", "arbitrary"),
        vmem_limit_bytes=int(2 * blocks + bias_bytes + temporaries + (8 << 20)))
    return pl.pallas_call(
        functools.partial(_na_kernel, seq=seq, row_slots=row_slots),
        out_shape=jax.ShapeDtypeStruct((b, seq, NA_WIDTH), BF16),
        grid=(NA_HEADS, b, seq // tq),
        in_specs=[
            pl.BlockSpec((None, tq, hd), lambda h, bi, s: (bi, s, OFF_NA_Q // hd + h)),
            pl.BlockSpec((None, seq, hd), lambda h, bi, s: (bi, 0, OFF_NA_K // hd + h)),
            pl.BlockSpec((None, seq, hd), lambda h, bi, s: (bi, 0, OFF_NA_V // hd + h)),
            pl.BlockSpec((None, NA_N_CHUNKS, NA_INVALID_SLOT + 1, NA_CQ, NA_LANES),
                         lambda h, bi, s: (h, 0, 0, 0, 0)),
        ],
        out_specs=pl.BlockSpec((None, tq, hd), lambda h, bi, s: (bi, s, h)),
        scratch_shapes=[pltpu.VMEM((3, NA_N_CHUNKS, NA_CHUNK_Q, NA_CHUNK_K), F32)],
        compiler_params=params,
        name="neighbourhood_attention",
    )(z3, z3, z3, col_table)


RET_CHUNK = 256
RET_TB = 2048
RET_NC = RET_TB // RET_CHUNK


def _rope_tables(seq):
    half = RET_QK_DIM // 2
    inv_freq = ROPE_BASE ** (-jnp.arange(half, dtype=F32) / half)
    ang = jnp.arange(seq, dtype=F32)[:, None] * inv_freq[None, :]
    cos, sin = jnp.cos(ang), jnp.sin(ang)
    return jnp.concatenate([cos, cos], axis=-1), jnp.concatenate([-sin, sin], axis=-1)


def _decay_tables(decay, backward):
    c = RET_CHUNK
    lg = jax.nn.log_sigmoid(decay.astype(F32))
    pos = jnp.arange(c, dtype=F32)
    diff = pos[:, None] - pos[None, :]
    lg3 = lg[:, None, None]
    if backward:
        decay_in = jnp.exp(jnp.where(diff < 0, -diff * lg3, -jnp.inf))
        q_dec = jnp.exp((c - pos)[None, :] * lg[:, None])
        k_dec = jnp.exp(pos[None, :] * lg[:, None])
    else:
        decay_in = jnp.exp(jnp.where(diff >= 0, diff * lg3, -jnp.inf))
        q_dec = jnp.exp((pos + 1.0)[None, :] * lg[:, None])
        k_dec = jnp.exp((c - 1.0 - pos)[None, :] * lg[:, None])
    chunk_dec = jnp.exp(c * lg)
    return decay_in, q_dec, k_dec, chunk_dec


def _retention_tables(decay_fwd, decay_bwd):
    c, dk = RET_CHUNK, RET_QK_DIM
    din_f, qd_f, kd_f, cd_f = _decay_tables(decay_fwd, False)
    din_b, qd_b, kd_b, cd_b = _decay_tables(decay_bwd, True)
    qd2 = jnp.concatenate([jnp.broadcast_to(qd_f[:, :, None], (RET_HEADS, c, dk)),
                           jnp.broadcast_to(qd_b[:, :, None], (RET_HEADS, c, dk))], axis=2)
    kdt2 = jnp.concatenate([jnp.broadcast_to(kd_f[:, None, :], (RET_HEADS, dk, c)),
                            jnp.broadcast_to(kd_b[:, None, :], (RET_HEADS, dk, c))], axis=1)
    return din_f + din_b, qd2, kdt2, jnp.stack([cd_f, cd_b])


def _retention_kernel(cd_ref, q_ref, k_ref, v_ref, cos_ref, sin_ref, din_ref, qd_ref, kdt_ref, g_ref,
                      o_ref, statef_ref, stateb_ref, kt_ref, kvf_ref, sb_ref, vc_ref, *, n_blk):
    h = pl.program_id(1)
    pass_id = pl.program_id(2)
    j = pl.program_id(3)
    dk, half = RET_QK_DIM, RET_QK_DIM // 2
    chunks = range(RET_NC)
    slices = [slice(c * RET_CHUNK, (c + 1) * RET_CHUNK) for c in chunks]

    @pl.when(j == 0)
    def _():
        statef_ref[...] = jnp.zeros_like(statef_ref)
        stateb_ref[...] = jnp.zeros_like(stateb_ref)

    def block_rows(blk):
        return pl.ds(pl.multiple_of(blk * RET_TB, RET_TB), RET_TB)

    def rotate(ref, blk):
        x = ref[...].astype(F32)
        return x * cos_ref[block_rows(blk), :] + pltpu.roll(x, half, 1) * sin_ref[block_rows(blk), :]

    @pl.when(pass_id == 0)
    def _():
        blk = n_blk - 1 - j
        kt = (rotate(k_ref, blk) * (dk ** -0.5)).T
        kt_ref[blk] = kt.astype(BF16)
        vc_ref[block_rows(blk), :] = v_ref[...]
        kdt2 = kdt_ref[...]
        kv = [jnp.dot((jnp.concatenate([kt[:, sl], kt[:, sl]], axis=0) * kdt2).astype(BF16), v_ref[sl, :],
                      preferred_element_type=F32) for sl in slices]
        cd_b = cd_ref[1, h]
        state = stateb_ref[...]
        for c in reversed(chunks):
            sb_ref[blk * RET_NC + c] = state.astype(BF16)
            kvf_ref[blk * RET_NC + c] = kv[c][:dk]
            state = cd_b * state + kv[c][dk:]
        stateb_ref[...] = state

    @pl.when(pass_id == 1)
    def _():
        blk = j
        q = rotate(q_ref, blk)
        kt = kt_ref[blk]
        din, qd2 = din_ref[...], qd_ref[...]
        scores = [jnp.dot(q[sl].astype(BF16), kt[:, sl], preferred_element_type=F32) for sl in slices]
        decayed = [(s * din).astype(BF16) for s in scores]
        cd_f = cd_ref[0, h]
        states = [statef_ref[...]]
        for c in chunks:
            states.append(cd_f * states[-1] + kvf_ref[blk * RET_NC + c])
        statef_ref[...] = states[-1]
        for c, sl in zip(chunks, slices):
            lhs = jnp.concatenate(
                [decayed[c], (jnp.concatenate([q[sl], q[sl]], axis=1) * qd2).astype(BF16)], axis=1)
            v_rows = pl.ds(pl.multiple_of(blk * RET_TB + c * RET_CHUNK, RET_CHUNK), RET_CHUNK)
            rhs = jnp.concatenate(
                [vc_ref[v_rows, :], states[c].astype(BF16), sb_ref[blk * RET_NC + c]], axis=0)
            o = jnp.dot(lhs, rhs, preferred_element_type=F32)
            o = o * lax.rsqrt(jnp.mean(o * o, axis=-1, keepdims=True) + EPS)
            g = g_ref[sl, :].astype(F32)
            o_ref[sl, :] = (o * (g * jax.nn.sigmoid(g))).astype(o_ref.dtype)


def _retention(z3, rope, tables):
    b, seq, _ = z3.shape
    n_blk = seq // RET_TB
    dk, dv, c = RET_QK_DIM, RET_V_DIM, RET_CHUNK
    cos2, sin2 = rope
    din, qd, kdt, cd = tables

    def pass1_blk(p, j):
        return jnp.where(p == 0, 0, j)

    def pass0_blk(p, j):
        return jnp.where(p == 0, n_blk - 1 - j, 0)

    n_chunks = seq // c
    blocks = RET_TB * (2 * dk * 2 + dv * 2 + 2 * dv * 2) + 2 * seq * dk * 4 + 3 * c * c * 4
    scratch_bytes = 2 * dk * dv * 4 + seq * dk * 2 + n_chunks * dk * dv * (4 + 2) + seq * dv * 2
    temporaries = 6 * RET_TB * dk * 4 + 16 * c * c * 4
    params = pltpu.CompilerParams(
        dimension_semantics=("parallel", "parallel", "arbitrary", "arbitrary"),
        vmem_limit_bytes=int(2 * blocks + scratch_bytes + temporaries + (8 << 20)))
    return pl.pallas_call(
        functools.partial(_retention_kernel, n_blk=n_blk),
        out_shape=jax.ShapeDtypeStruct((b, seq, RET_V_WIDTH), BF16),
        grid=(b, RET_HEADS, 2, n_blk),
        in_specs=[
            pl.BlockSpec(memory_space=pltpu.SMEM),
            pl.BlockSpec((None, RET_TB, dk), lambda bi, h, p, j: (bi, pass1_blk(p, j), OFF_RET_Q // dk + h)),
            pl.BlockSpec((None, RET_TB, dk), lambda bi, h, p, j: (bi, pass0_blk(p, j), OFF_RET_K // dk + h)),
            pl.BlockSpec((None, RET_TB, dv), lambda bi, h, p, j: (bi, pass0_blk(p, j), OFF_RET_V // dv + h)),
            pl.BlockSpec((seq, dk), lambda bi, h, p, j: (0, 0)),
            pl.BlockSpec((seq, dk), lambda bi, h, p, j: (0, 0)),
            pl.BlockSpec((None, c, c), lambda bi, h, p, j: (h, 0, 0)),
            pl.BlockSpec((None, c, 2 * dk), lambda bi, h, p, j: (h, 0, 0)),
            pl.BlockSpec((None, 2 * dk, c), lambda bi, h, p, j: (h, 0, 0)),
            pl.BlockSpec((None, RET_TB, dv), lambda bi, h, p, j: (bi, pass1_blk(p, j), OFF_RET_G // dv + h)),
        ],
        out_specs=pl.BlockSpec((None, RET_TB, dv), lambda bi, h, p, j: (bi, pass1_blk(p, j), h)),
        scratch_shapes=[
            pltpu.VMEM((dk, dv), F32),
            pltpu.VMEM((dk, dv), F32),
            pltpu.VMEM((n_blk, dk, RET_TB), BF16),
            pltpu.VMEM((n_chunks, dk, dv), F32),
            pltpu.VMEM((n_chunks, dk, dv), BF16),
            pltpu.VMEM((seq, dv), BF16),
        ],
        compiler_params=params,
        name="retention",
    )(cd, z3, z3, z3, cos2, sin2, din, qd, kdt, z3)


MIX_TM = 1024
MIX_TN = 1024


def _mix_kernel(a_ref, r_ref, wna_ref, wret_ref, ga_ref, gb_ref, m_ref):
    a = jnp.dot(a_ref[...], wna_ref[...], preferred_element_type=F32)
    r = jnp.dot(r_ref[...], wret_ref[...], preferred_element_type=F32)
    ga = jax.nn.sigmoid(ga_ref[...].astype(F32))
    gb = jax.nn.sigmoid(gb_ref[...].astype(F32))
    m_ref[...] = (ga * a + gb * r).astype(m_ref.dtype)


def _branch_mix(a2d, r2d, z2d, w_na_out, w_ret_out):
    m = a2d.shape[0]
    tm, tn = MIX_TM, MIX_TN
    blocks = tm * NA_WIDTH * 2 + tm * RET_V_WIDTH * 2 + (NA_WIDTH + RET_V_WIDTH) * tn * 2 + 3 * tm * tn * 2
    blocks += 4 * tm * tn * 4
    return pl.pallas_call(
        _mix_kernel,
        out_shape=jax.ShapeDtypeStruct((m, D_MODEL), BF16),
        grid=(m // tm, D_MODEL // tn),
        in_specs=[
            pl.BlockSpec((tm, NA_WIDTH), lambda i, j: (i, 0)),
            pl.BlockSpec((tm, RET_V_WIDTH), lambda i, j: (i, 0)),
            pl.BlockSpec((NA_WIDTH, tn), lambda i, j: (0, j)),
            pl.BlockSpec((RET_V_WIDTH, tn), lambda i, j: (0, j)),
            pl.BlockSpec((tm, tn), lambda i, j: (i, OFF_GATE_A // tn + j)),
            pl.BlockSpec((tm, tn), lambda i, j: (i, OFF_GATE_B // tn + j)),
        ],
        out_specs=pl.BlockSpec((tm, tn), lambda i, j: (i, j)),
        compiler_params=_compiler_params(("parallel", "arbitrary"), blocks),
        name="branch_mix",
    )(a2d, r2d, w_na_out, w_ret_out, z2d, z2d)


OUT_TM = 1024
OUT_TN = 1024


def _outproj_kernel(x_ref, m_ref, w_ref, h_ref):
    h_ref[...] = x_ref[...] + jnp.dot(m_ref[...], w_ref[...], preferred_element_type=F32)


def _outproj(x2d, m2d, w_out):
    m = x2d.shape[0]
    tm, tn = OUT_TM, OUT_TN
    blocks = 2 * tm * tn * 4 + tm * D_MODEL * 2 + D_MODEL * tn * 2 + tm * tn * 4
    return pl.pallas_call(
        _outproj_kernel,
        out_shape=jax.ShapeDtypeStruct((m, D_MODEL), F32),
        grid=(m // tm, D_MODEL // tn),
        in_specs=[
            pl.BlockSpec((tm, tn), lambda i, j: (i, j)),
            pl.BlockSpec((tm, D_MODEL), lambda i, j: (i, 0)),
            pl.BlockSpec((D_MODEL, tn), lambda i, j: (0, j)),
        ],
        out_specs=pl.BlockSpec((tm, tn), lambda i, j: (i, j)),
        compiler_params=_compiler_params(("parallel", "arbitrary"), blocks),
        name="outproj_residual",
    )(x2d, m2d, w_out)


MLP_TM = 512
MLP_TF = 1024


def _mlp_kernel(h_ref, nw_ref, wup_ref, wdown_ref, fw_ref, y_ref, hn_ref, acc_ref):
    f = pl.program_id(1)

    @pl.when(f == 0)
    def _():
        h = h_ref[...]
        ms = jnp.mean(h * h, axis=-1, keepdims=True)
        hn_ref[...] = (h * lax.rsqrt(ms + EPS) * nw_ref[...]).astype(BF16)
        acc_ref[...] = h

    u = jnp.maximum(jnp.dot(hn_ref[...], wup_ref[...], preferred_element_type=F32), 0.0)
    acc_ref[...] += jnp.dot((u * u).astype(BF16), wdown_ref[...], preferred_element_type=F32)

    @pl.when(f == pl.num_programs(1) - 1)
    def _():
        x = acc_ref[...]
        ms = jnp.mean(x * x, axis=-1, keepdims=True)
        y_ref[...] = x * lax.rsqrt(ms + EPS) * fw_ref[...]


def _mlp(h2d, norm_w, w_up, w_down, final_w):
    m = h2d.shape[0]
    tm, tf = MLP_TM, MLP_TF
    blocks = 2 * tm * D_MODEL * 4 + 2 * D_MODEL * tf * 2 + tm * D_MODEL * 2 + tm * D_MODEL * 4 + 2 * tm * tf * 4
    return pl.pallas_call(
        _mlp_kernel,
        out_shape=jax.ShapeDtypeStruct((m, D_MODEL), F32),
        grid=(m // tm, D_FF // tf),
        in_specs=[
            pl.BlockSpec((tm, D_MODEL), lambda i, f: (i, 0)),
            pl.BlockSpec((1, D_MODEL), lambda i, f: (0, 0)),
            pl.BlockSpec((D_MODEL, tf), lambda i, f: (0, f)),
            pl.BlockSpec((tf, D_MODEL), lambda i, f: (f, 0)),
            pl.BlockSpec((1, D_MODEL), lambda i, f: (0, 0)),
        ],
        out_specs=pl.BlockSpec((tm, D_MODEL), lambda i, f: (i, 0)),
        scratch_shapes=[pltpu.VMEM((tm, D_MODEL), BF16), pltpu.VMEM((tm, D_MODEL), F32)],
        compiler_params=_compiler_params(("parallel", "arbitrary"), blocks),
        name="mlp_final_norm",
    )(h2d, norm_w.reshape(1, D_MODEL), w_up, w_down, final_w.reshape(1, D_MODEL))


def _encode(x, p):
    b, seq, d = x.shape
    x2d = x.reshape(b * seq, d)
    z2d = _inproj(x2d, p["norm_mix_w"], p["w_in"])
    z3 = z2d.reshape(b, seq, IN_COLS)
    a = _neighbourhood_attention(z3, p["na_bias"])
    r = _retention(z3, p["rope"], p["ret_tables"])
    m2d = _branch_mix(a.reshape(b * seq, NA_WIDTH), r.reshape(b * seq, RET_V_WIDTH), z2d,
                      p["w_na_out"], p["w_ret_out"])
    h2d = _outproj(x2d, m2d, p["w_out"])
    y2d = _mlp(h2d, p["norm_mlp_w"], p["w_mlp_up"], p["w_mlp_down"], p["norm_final_w"])
    return y2d.reshape(b, seq, d)


def kernel(x_prompt, x_sample, norm_mix_w, w_in, na_rpb, ret_decay_fwd, ret_decay_bwd, w_na_out, w_ret_out,
           w_out, norm_mlp_w, w_mlp_up, w_mlp_down, norm_final_w):
    depth = w_in.shape[0]
    seq = x_prompt.shape[1]
    assert x_sample.shape[1] == seq and seq % RET_TB == 0 and seq % (NA_SUB * NA_TQ) == 0
    assert depth == 1, "the final norm is fused into the last layer's MLP kernel"
    rope = _rope_tables(seq)
    l = 0
    params = {
        "norm_mix_w": norm_mix_w[l].astype(F32),
        "w_in": w_in[l].astype(BF16),
        "na_bias": _na_column_table(na_rpb[l]),
        "rope": rope,
        "ret_tables": _retention_tables(ret_decay_fwd[l], ret_decay_bwd[l]),
        "w_na_out": w_na_out[l].astype(BF16),
        "w_ret_out": w_ret_out[l].astype(BF16),
        "w_out": w_out[l].astype(BF16),
        "norm_mlp_w": norm_mlp_w[l].astype(F32),
        "w_mlp_up": w_mlp_up[l].astype(BF16),
        "w_mlp_down": w_mlp_down[l].astype(BF16),
        "norm_final_w": norm_final_w.astype(F32),
    }
    return _encode(x_prompt, params), _encode(x_sample, params)
```

```python
import functools

import jax
import jax.numpy as jnp
import numpy as np
from jax import lax
from jax.experimental import pallas as pl
from jax.experimental.pallas import tpu as pltpu

F32 = jnp.float32
BF16 = jnp.bfloat16

D_MODEL = 2048
GRID_W = 64
NA_HEADS = 8
NA_HEAD_DIM = 128
NA_WIDTH = NA_HEADS * NA_HEAD_DIM
NA_WIN_ROWS = 8
NA_WIN_COLS = 16
RET_HEADS = 8
RET_QK_DIM = 128
RET_V_DIM = 256
RET_QK_WIDTH = RET_HEADS * RET_QK_DIM
RET_V_WIDTH = RET_HEADS * RET_V_DIM
ROPE_BASE = 10000.0
D_FF = 4 * D_MODEL
EPS = 1e-6

OFF_NA_Q = 0
OFF_NA_K = OFF_NA_Q + NA_WIDTH
OFF_NA_V = OFF_NA_K + NA_WIDTH
OFF_RET_Q = OFF_NA_V + NA_WIDTH
OFF_RET_K = OFF_RET_Q + RET_QK_WIDTH
OFF_RET_V = OFF_RET_K + RET_QK_WIDTH
OFF_RET_G = OFF_RET_V + RET_V_WIDTH
OFF_GATE_A = OFF_RET_G + RET_V_WIDTH
OFF_GATE_B = OFF_GATE_A + D_MODEL
IN_COLS = OFF_GATE_B + D_MODEL

MASK_VALUE = -1e30
V7X_VMEM_BYTES = 64 * 1024 * 1024


def _compiler_params(semantics, block_bytes):
    limit = min(2 * block_bytes + (8 << 20), V7X_VMEM_BYTES - (4 << 20))
    return pltpu.CompilerParams(dimension_semantics=semantics, vmem_limit_bytes=int(limit))


IN_TM = 1024
IN_TN = 1024


def _inproj_kernel(x_ref, nw_ref, w_ref, z_ref, xn_ref):
    @pl.when(pl.program_id(1) == 0)
    def _():
        x = x_ref[...]
        ms = jnp.mean(x * x, axis=-1, keepdims=True)
        xn_ref[...] = (x * lax.rsqrt(ms + EPS) * nw_ref[...]).astype(BF16)

    z_ref[...] = jnp.dot(xn_ref[...], w_ref[...], preferred_element_type=F32).astype(z_ref.dtype)


def _inproj(x2d, norm_w, w_in):
    m = x2d.shape[0]
    tm, tn = IN_TM, IN_TN
    blocks = tm * D_MODEL * 4 + D_MODEL * tn * 2 + tm * tn * 2 + tm * D_MODEL * 2
    return pl.pallas_call(
        _inproj_kernel,
        out_shape=jax.ShapeDtypeStruct((m, IN_COLS), BF16),
        grid=(m // tm, IN_COLS // tn),
        in_specs=[
            pl.BlockSpec((tm, D_MODEL), lambda i, j: (i, 0)),
            pl.BlockSpec((1, D_MODEL), lambda i, j: (0, 0)),
            pl.BlockSpec((D_MODEL, tn), lambda i, j: (0, j)),
        ],
        out_specs=pl.BlockSpec((tm, tn), lambda i, j: (i, j)),
        scratch_shapes=[pltpu.VMEM((tm, D_MODEL), BF16)],
        compiler_params=_compiler_params(("parallel", "arbitrary"), blocks),
        name="norm_inproj",
    )(x2d, norm_w.reshape(1, D_MODEL), w_in)


NA_SUB = 4
NA_Q_ROWS = 8
LOG2_E = 1.4426950408889634
NA_Q_SCALE = NA_HEAD_DIM ** -0.5 * LOG2_E
NA_K_ROWS = 16
NA_TQ = NA_Q_ROWS * GRID_W
NA_TK = NA_K_ROWS * GRID_W
NA_CQ = 16
NA_CK = 32
NA_N_CHUNKS = GRID_W // NA_CQ
NA_CHUNK_Q = NA_Q_ROWS * NA_CQ
NA_CHUNK_K = NA_K_ROWS * NA_CK
SUBLANES = 8


def _na_key_col_starts():
    starts = []
    for cq in range(NA_N_CHUNKS):
        c = np.arange(cq * NA_CQ, (cq + 1) * NA_CQ)
        c0 = np.clip(c - NA_WIN_COLS // 2, 0, GRID_W - NA_WIN_COLS)
        start = min((int(c0.min()) // SUBLANES) * SUBLANES, GRID_W - NA_CK)
        assert start <= c0.min() and c0.max() + NA_WIN_COLS <= start + NA_CK
        starts.append(start)
    return tuple(starts)


NA_KEY_COL_START = _na_key_col_starts()


NA_INVALID_SLOT = 2 * NA_WIN_ROWS - 1
NA_LANES = 128
NA_LANE_REP = NA_LANES // NA_CK


def _na_row_slots(rows):
    ri = np.arange(NA_Q_ROWS)
    kri = np.arange(NA_K_ROWS)
    half = NA_WIN_ROWS // 2
    n_blocks = rows // NA_Q_ROWS
    row_idx = []
    for kind in range(3):
        q_base = {0: 0, 1: NA_Q_ROWS, 2: (n_blocks - 1) * NA_Q_ROWS}[kind]
        k_base = int(np.clip(q_base - half, 0, rows - NA_K_ROWS))
        r = q_base + ri
        r0 = np.clip(r - half, 0, rows - NA_WIN_ROWS)
        kr = k_base + kri
        valid = (kr[None, :] >= r0[:, None]) & (kr[None, :] < r0[:, None] + NA_WIN_ROWS)
        idx = kr[None, :] - r[:, None] + NA_WIN_ROWS - 1
        row_idx.append(np.where(valid, idx, NA_INVALID_SLOT))
    return np.stack(row_idx)


def _na_column_table(rpb):
    cl = np.arange(NA_CQ)
    kcl = np.arange(NA_CK)
    c = (np.arange(NA_N_CHUNKS) * NA_CQ)[:, None, None] + cl[None, :, None]
    kc = np.asarray(NA_KEY_COL_START)[:, None, None] + kcl[None, None, :]
    c0 = np.clip(c - NA_WIN_COLS // 2, 0, GRID_W - NA_WIN_COLS)
    col_valid = (kc >= c0) & (kc < c0 + NA_WIN_COLS)
    col_idx = np.clip(kc - c + NA_WIN_COLS - 1, 0, 2 * NA_WIN_COLS - 2)
    bias = rpb.astype(F32)[:, :, col_idx] * LOG2_E
    tc = jnp.where(col_valid[None, None], bias, MASK_VALUE)
    tc = jnp.concatenate([tc, jnp.full((NA_HEADS, 1) + tc.shape[2:], MASK_VALUE, F32)], axis=1)
    tc = jnp.transpose(tc, (0, 2, 1, 3, 4))
    return jnp.tile(tc, (1, 1, 1, 1, NA_LANE_REP))


def _na_build_bias(tc_ref, bias_ref, row_slots):
    lane = lax.broadcasted_iota(jnp.int32, (NA_CQ, NA_LANES), 1)
    for kind in range(3):
        for cq in range(NA_N_CHUNKS):
            for ri in range(NA_Q_ROWS):
                for grp in range(NA_K_ROWS // NA_LANE_REP):
                    slots = [int(row_slots[kind, ri, grp * NA_LANE_REP + t]) for t in range(NA_LANE_REP)]
                    tile = tc_ref[cq, slots[-1]]
                    for t in range(NA_LANE_REP - 2, -1, -1):
                        if slots[t] != slots[t + 1]:
                            tile = jnp.where(lane < (t + 1) * NA_CK, tc_ref[cq, slots[t]], tile)
                    bias_ref[kind, cq, ri * NA_CQ:(ri + 1) * NA_CQ, grp * NA_LANES:(grp + 1) * NA_LANES] = tile


def _na_kernel(q_ref, k_ref, v_ref, tc_ref, o_ref, bias_ref, *, seq, row_slots):
    hd = NA_HEAD_DIM
    half_rows = (NA_WIN_ROWS // 2) * GRID_W
    n_blocks = seq // NA_TQ

    @pl.when((pl.program_id(1) == 0) & (pl.program_id(2) == 0))
    def _():
        _na_build_bias(tc_ref, bias_ref, row_slots)

    pairs = [(sb, cq) for sb in range(NA_SUB) for cq in range(NA_N_CHUNKS)]
    qs, ks, vs, kinds = [], [], [], []
    for sb in range(NA_SUB):
        blk = pl.program_id(2) * NA_SUB + sb
        start = pl.multiple_of(jnp.clip(blk * NA_TQ - half_rows, 0, seq - NA_TK), half_rows)
        q = q_ref[sb * NA_TQ:(sb + 1) * NA_TQ, :].astype(F32) * NA_Q_SCALE
        qs.append(q.reshape(NA_Q_ROWS, GRID_W, hd))
        ks.append(k_ref[pl.ds(start, NA_TK), :].astype(F32).reshape(NA_K_ROWS, GRID_W, hd))
        vs.append(v_ref[pl.ds(start, NA_TK), :].astype(F32).reshape(NA_K_ROWS, GRID_W, hd))
        kinds.append(jnp.where(blk == 0, 0, jnp.where(blk == n_blocks - 1, 2, 1)))
    scores = []
    for sb, cq in pairs:
        k0 = NA_KEY_COL_START[cq]
        qc = qs[sb][:, cq * NA_CQ:(cq + 1) * NA_CQ, :].reshape(NA_CHUNK_Q, hd).astype(BF16)
        kc = ks[sb][:, k0:k0 + NA_CK, :].reshape(NA_CHUNK_K, hd).astype(BF16)
        scores.append(lax.dot_general(qc, kc, (((1,), (1,)), ((), ())), preferred_element_type=F32))
    probs, denoms = [], []
    for (sb, cq), s in zip(pairs, scores):
        s = s + bias_ref[kinds[sb], cq]
        p = jnp.exp2(s - jnp.max(s, axis=-1, keepdims=True))
        denoms.append(jnp.sum(p, axis=-1, keepdims=True))
        probs.append(p.astype(BF16))
    outs = []
    for (sb, cq), p, l in zip(pairs, probs, denoms):
        k0 = NA_KEY_COL_START[cq]
        vc = vs[sb][:, k0:k0 + NA_CK, :].reshape(NA_CHUNK_K, hd).astype(BF16)
        o = jnp.dot(p, vc, preferred_element_type=F32) / l
        outs.append(o.reshape(NA_Q_ROWS, NA_CQ, hd))
    for sb in range(NA_SUB):
        o = jnp.concatenate(outs[sb * NA_N_CHUNKS:(sb + 1) * NA_N_CHUNKS], axis=1)
        o_ref[sb * NA_TQ:(sb + 1) * NA_TQ, :] = o.reshape(NA_TQ, hd).astype(o_ref.dtype)


def _neighbourhood_attention(z3, col_table):
    b, seq, _ = z3.shape
    tq = NA_SUB * NA_TQ
    hd = NA_HEAD_DIM
    row_slots = _na_row_slots(seq // GRID_W)
    table_bytes = NA_N_CHUNKS * (NA_INVALID_SLOT + 1) * NA_CQ * NA_LANES * 4
    bias_bytes = 3 * NA_N_CHUNKS * NA_CHUNK_Q * NA_CHUNK_K * 4
    blocks = tq * hd * 2 * 2 + 2 * seq * hd * 2 + table_bytes
    temporaries = NA_SUB * (3 * NA_TK * hd * 4 + 3 * NA_N_CHUNKS * NA_CHUNK_Q * NA_CHUNK_K * 4)
    params = pltpu.CompilerParams(
        dimension_semantics=("arbitrary", "arbitrary", "arbitrary"),
        vmem_limit_bytes=int(2 * blocks + bias_bytes + temporaries + (8 << 20)))
    return pl.pallas_call(
        functools.partial(_na_kernel, seq=seq, row_slots=row_slots),
        out_shape=jax.ShapeDtypeStruct((b, seq, NA_WIDTH), BF16),
        grid=(NA_HEADS, b, seq // tq),
        in_specs=[
            pl.BlockSpec((None, tq, hd), lambda h, bi, s: (bi, s, OFF_NA_Q // hd + h)),
            pl.BlockSpec((None, seq, hd), lambda h, bi, s: (bi, 0, OFF_NA_K // hd + h)),
            pl.BlockSpec((None, seq, hd), lambda h, bi, s: (bi, 0, OFF_NA_V // hd + h)),
            pl.BlockSpec((None, NA_N_CHUNKS, NA_INVALID_SLOT + 1, NA_CQ, NA_LANES),
                         lambda h, bi, s: (h, 0, 0, 0, 0)),
        ],
        out_specs=pl.BlockSpec((None, tq, hd), lambda h, bi, s: (bi, s, h)),
        scratch_shapes=[pltpu.VMEM((3, NA_N_CHUNKS, NA_CHUNK_Q, NA_CHUNK_K), F32)],
        compiler_params=params,
        name="neighbourhood_attention",
    )(z3, z3, z3, col_table)


RET_CHUNK = 256
RET_TB = 2048
RET_NC = RET_TB // RET_CHUNK


def _rope_tables(seq):
    half = RET_QK_DIM // 2
    inv_freq = ROPE_BASE ** (-jnp.arange(half, dtype=F32) / half)
    ang = jnp.arange(seq, dtype=F32)[:, None] * inv_freq[None, :]
    cos, sin = jnp.cos(ang), jnp.sin(ang)
    return jnp.concatenate([cos, cos], axis=-1), jnp.concatenate([-sin, sin], axis=-1)


def _decay_tables(decay, backward):
    c = RET_CHUNK
    lg = jax.nn.log_sigmoid(decay.astype(F32))
    pos = jnp.arange(c, dtype=F32)
    diff = pos[:, None] - pos[None, :]
    lg3 = lg[:, None, None]
    if backward:
        decay_in = jnp.exp(jnp.where(diff < 0, -diff * lg3, -jnp.inf))
        q_dec = jnp.exp((c - pos)[None, :] * lg[:, None])
        k_dec = jnp.exp(pos[None, :] * lg[:, None])
    else:
        decay_in = jnp.exp(jnp.where(diff >= 0, diff * lg3, -jnp.inf))
        q_dec = jnp.exp((pos + 1.0)[None, :] * lg[:, None])
        k_dec = jnp.exp((c - 1.0 - pos)[None, :] * lg[:, None])
    chunk_dec = jnp.exp(c * lg)
    return decay_in, q_dec, k_dec, chunk_dec


def _retention_tables(decay_fwd, decay_bwd):
    c, dk = RET_CHUNK, RET_QK_DIM
    din_f, qd_f, kd_f, cd_f = _decay_tables(decay_fwd, False)
    din_b, qd_b, kd_b, cd_b = _decay_tables(decay_bwd, True)
    qd2 = jnp.concatenate([jnp.broadcast_to(qd_f[:, :, None], (RET_HEADS, c, dk)),
                           jnp.broadcast_to(qd_b[:, :, None], (RET_HEADS, c, dk))], axis=2)
    kdt2 = jnp.concatenate([jnp.broadcast_to(kd_f[:, None, :], (RET_HEADS, dk, c)),
                            jnp.broadcast_to(kd_b[:, None, :], (RET_HEADS, dk, c))], axis=1)
    return din_f + din_b, qd2, kdt2, jnp.stack([cd_f, cd_b])


def _retention_kernel(cd_ref, q_ref, k_ref, v_ref, cos_ref, sin_ref, din_ref, qd_ref, kdt_ref, g_ref,
                      o_ref, statef_ref, stateb_ref, kt_ref, kvf_ref, sb_ref, vc_ref, *, n_blk):
    h = pl.program_id(1)
    pass_id = pl.program_id(2)
    j = pl.program_id(3)
    dk, half = RET_QK_DIM, RET_QK_DIM // 2
    chunks = range(RET_NC)
    slices = [slice(c * RET_CHUNK, (c + 1) * RET_CHUNK) for c in chunks]

    @pl.when(j == 0)
    def _():
        statef_ref[...] = jnp.zeros_like(statef_ref)
        stateb_ref[...] = jnp.zeros_like(stateb_ref)

    def block_rows(blk):
        return pl.ds(pl.multiple_of(blk * RET_TB, RET_TB), RET_TB)

    def rotate(ref, blk):
        x = ref[...].astype(F32)
        return x * cos_ref[block_rows(blk), :] + pltpu.roll(x, half, 1) * sin_ref[block_rows(blk), :]

    @pl.when(pass_id == 0)
    def _():
        blk = n_blk - 1 - j
        kt = (rotate(k_ref, blk) * (dk ** -0.5)).T
        kt_ref[blk] = kt.astype(BF16)
        vc_ref[block_rows(blk), :] = v_ref[...]
        kdt2 = kdt_ref[...]
        kv = [jnp.dot((jnp.concatenate([kt[:, sl], kt[:, sl]], axis=0) * kdt2).astype(BF16), v_ref[sl, :],
                      preferred_element_type=F32) for sl in slices]
        cd_b = cd_ref[1, h]
        state = stateb_ref[...]
        for c in reversed(chunks):
            sb_ref[blk * RET_NC + c] = state.astype(BF16)
            kvf_ref[blk * RET_NC + c] = kv[c][:dk]
            state = cd_b * state + kv[c][dk:]
        stateb_ref[...] = state

    @pl.when(pass_id == 1)
    def _():
        blk = j
        q = rotate(q_ref, blk)
        kt = kt_ref[blk]
        din, qd2 = din_ref[...], qd_ref[...]
        scores = [jnp.dot(q[sl].astype(BF16), kt[:, sl], preferred_element_type=F32) for sl in slices]
        decayed = [(s * din).astype(BF16) for s in scores]
        cd_f = cd_ref[0, h]
        states = [statef_ref[...]]
        for c in chunks:
            states.append(cd_f * states[-1] + kvf_ref[blk * RET_NC + c])
        statef_ref[...] = states[-1]
        for c, sl in zip(chunks, slices):
            lhs = jnp.concatenate(
                [decayed[c], (jnp.concatenate([q[sl], q[sl]], axis=1) * qd2).astype(BF16)], axis=1)
            v_rows = pl.ds(pl.multiple_of(blk * RET_TB + c * RET_CHUNK, RET_CHUNK), RET_CHUNK)
            rhs = jnp.concatenate(
                [vc_ref[v_rows, :], states[c].astype(BF16), sb_ref[blk * RET_NC + c]], axis=0)
            o = jnp.dot(lhs, rhs, preferred_element_type=F32)
            o = o * lax.rsqrt(jnp.mean(o * o, axis=-1, keepdims=True) + EPS)
            g = g_ref[sl, :].astype(F32)
            o_ref[sl, :] = (o * (g * jax.nn.sigmoid(g))).astype(o_ref.dtype)


def _retention(z3, rope, tables):
    b, seq, _ = z3.shape
    n_blk = seq // RET_TB
    dk, dv, c = RET_QK_DIM, RET_V_DIM, RET_CHUNK
    cos2, sin2 = rope
    din, qd, kdt, cd = tables

    def pass1_blk(p, j):
        return jnp.where(p == 0, 0, j)

    def pass0_blk(p, j):
        return jnp.where(p == 0, n_blk - 1 - j, 0)

    n_chunks = seq // c
    blocks = RET_TB * (2 * dk * 2 + dv * 2 + 2 * dv * 2) + 2 * seq * dk * 4 + 3 * c * c * 4
    scratch_bytes = 2 * dk * dv * 4 + seq * dk * 2 + n_chunks * dk * dv * (4 + 2) + seq * dv * 2
    temporaries = 6 * RET_TB * dk * 4 + 16 * c * c * 4
    params = pltpu.CompilerParams(
        dimension_semantics=("parallel", "parallel", "arbitrary", "arbitrary"),
        vmem_limit_bytes=int(2 * blocks + scratch_bytes + temporaries + (8 << 20)))
    return pl.pallas_call(
        functools.partial(_retention_kernel, n_blk=n_blk),
        out_shape=jax.ShapeDtypeStruct((b, seq, RET_V_WIDTH), BF16),
        grid=(b, RET_HEADS, 2, n_blk),
        in_specs=[
            pl.BlockSpec(memory_space=pltpu.SMEM),
            pl.BlockSpec((None, RET_TB, dk), lambda bi, h, p, j: (bi, pass1_blk(p, j), OFF_RET_Q // dk + h)),
            pl.BlockSpec((None, RET_TB, dk), lambda bi, h, p, j: (bi, pass0_blk(p, j), OFF_RET_K // dk + h)),
            pl.BlockSpec((None, RET_TB, dv), lambda bi, h, p, j: (bi, pass0_blk(p, j), OFF_RET_V // dv + h)),
            pl.BlockSpec((seq, dk), lambda bi, h, p, j: (0, 0)),
            pl.BlockSpec((seq, dk), lambda bi, h, p, j: (0, 0)),
            pl.BlockSpec((None, c, c), lambda bi, h, p, j: (h, 0, 0)),
            pl.BlockSpec((None, c, 2 * dk), lambda bi, h, p, j: (h, 0, 0)),
            pl.BlockSpec((None, 2 * dk, c), lambda bi, h, p, j: (h, 0, 0)),
            pl.BlockSpec((None, RET_TB, dv), lambda bi, h, p, j: (bi, pass1_blk(p, j), OFF_RET_G // dv + h)),
        ],
        out_specs=pl.BlockSpec((None, RET_TB, dv), lambda bi, h, p, j: (bi, pass1_blk(p, j), h)),
        scratch_shapes=[
            pltpu.VMEM((dk, dv), F32),
            pltpu.VMEM((dk, dv), F32),
            pltpu.VMEM((n_blk, dk, RET_TB), BF16),
            pltpu.VMEM((n_chunks, dk, dv), F32),
            pltpu.VMEM((n_chunks, dk, dv), BF16),
            pltpu.VMEM((seq, dv), BF16),
        ],
        compiler_params=params,
        name="retention",
    )(cd, z3, z3, z3, cos2, sin2, din, qd, kdt, z3)


MIX_TM = 1024
MIX_TN = 1024


def _mix_kernel(a_ref, r_ref, wna_ref, wret_ref, ga_ref, gb_ref, m_ref):
    a = jnp.dot(a_ref[...], wna_ref[...], preferred_element_type=F32)
    r = jnp.dot(r_ref[...], wret_ref[...], preferred_element_type=F32)
    ga = jax.nn.sigmoid(ga_ref[...].astype(F32))
    gb = jax.nn.sigmoid(gb_ref[...].astype(F32))
    m_ref[...] = (ga * a + gb * r).astype(m_ref.dtype)


def _branch_mix(a2d, r2d, z2d, w_na_out, w_ret_out):
    m = a2d.shape[0]
    tm, tn = MIX_TM, MIX_TN
    blocks = tm * NA_WIDTH * 2 + tm * RET_V_WIDTH * 2 + (NA_WIDTH + RET_V_WIDTH) * tn * 2 + 3 * tm * tn * 2
    blocks += 4 * tm * tn * 4
    return pl.pallas_call(
        _mix_kernel,
        out_shape=jax.ShapeDtypeStruct((m, D_MODEL), BF16),
        grid=(m // tm, D_MODEL // tn),
        in_specs=[
            pl.BlockSpec((tm, NA_WIDTH), lambda i, j: (i, 0)),
            pl.BlockSpec((tm, RET_V_WIDTH), lambda i, j: (i, 0)),
            pl.BlockSpec((NA_WIDTH, tn), lambda i, j: (0, j)),
            pl.BlockSpec((RET_V_WIDTH, tn), lambda i, j: (0, j)),
            pl.BlockSpec((tm, tn), lambda i, j: (i, OFF_GATE_A // tn + j)),
            pl.BlockSpec((tm, tn), lambda i, j: (i, OFF_GATE_B // tn + j)),
        ],
        out_specs=pl.BlockSpec((tm, tn), lambda i, j: (i, j)),
        compiler_params=_compiler_params(("parallel", "arbitrary"), blocks),
        name="branch_mix",
    )(a2d, r2d, w_na_out, w_ret_out, z2d, z2d)


OUT_TM = 512
OUT_TN = D_MODEL


def _outproj_kernel(x_ref, m_ref, w_ref, h_ref):
    h_ref[...] = x_ref[...] + jnp.dot(m_ref[...], w_ref[...], preferred_element_type=F32)


def _outproj(x2d, m2d, w_out):
    m = x2d.shape[0]
    tm, tn = OUT_TM, OUT_TN
    blocks = 2 * tm * tn * 4 + tm * D_MODEL * 2 + D_MODEL * tn * 2 + tm * tn * 4
    return pl.pallas_call(
        _outproj_kernel,
        out_shape=jax.ShapeDtypeStruct((m, D_MODEL), F32),
        grid=(m // tm, D_MODEL // tn),
        in_specs=[
            pl.BlockSpec((tm, tn), lambda i, j: (i, j)),
            pl.BlockSpec((tm, D_MODEL), lambda i, j: (i, 0)),
            pl.BlockSpec((D_MODEL, tn), lambda i, j: (0, j)),
        ],
        out_specs=pl.BlockSpec((tm, tn), lambda i, j: (i, j)),
        compiler_params=_compiler_params(("parallel", "arbitrary"), blocks),
        name="outproj_residual",
    )(x2d, m2d, w_out)


MLP_TM = 512
MLP_TF = 1024


def _mlp_kernel(h_ref, nw_ref, wup_ref, wdown_ref, fw_ref, y_ref, hn0_ref, hn1_ref, acc0_ref, acc1_ref, *,
                n_tiles):
    i = pl.program_id(0)
    f = pl.program_id(1)
    slice_rows = h_ref.shape[0] // pl.num_programs(1)
    rows = pl.ds(pl.multiple_of(f * slice_rows, slice_rows), slice_rows)
    hn_refs, acc_refs = (hn0_ref, hn1_ref), (acc0_ref, acc1_ref)

    def final_norm_slice(acc_ref):
        x = acc_ref[rows, :]
        ms = jnp.mean(x * x, axis=-1, keepdims=True)
        y_ref[rows, :] = x * lax.rsqrt(ms + EPS) * fw_ref[...]

    def norm_slice(hn_ref, acc_ref):
        h = h_ref[rows, :]
        ms = jnp.mean(h * h, axis=-1, keepdims=True)
        hn_ref[rows, :] = (h * lax.rsqrt(ms + EPS) * nw_ref[...]).astype(BF16)
        acc_ref[rows, :] = h

    @pl.when(i == 0)
    def _():
        norm_slice(hn0_ref, acc0_ref)
        acc1_ref[rows, :] = jnp.zeros((slice_rows, acc1_ref.shape[1]), F32)

    for parity in range(2):
        @pl.when((i >= 1) & (i <= n_tiles) & (i % 2 == parity))
        def _():
            hn_new, acc_new = hn_refs[parity], acc_refs[parity]
            hn_mm, acc_mm = hn_refs[1 - parity], acc_refs[1 - parity]
            final_norm_slice(acc_new)
            norm_slice(hn_new, acc_new)
            u = jnp.maximum(jnp.dot(hn_mm[...], wup_ref[...], preferred_element_type=F32), 0.0)
            acc_mm[...] += jnp.dot((u * u).astype(BF16), wdown_ref[...], preferred_element_type=F32)

    @pl.when(i == n_tiles + 1)
    def _():
        final_norm_slice(acc_refs[(n_tiles + 1) % 2])


def _mlp(h2d, norm_w, w_up, w_down, final_w):
    m = h2d.shape[0]
    tm, tf = MLP_TM, MLP_TF
    n_tiles, n_ff = m // tm, D_FF // tf
    assert tm % n_ff == 0 and (tm // n_ff) % 16 == 0

    def ff_blk(i, f):
        return jnp.where(i == 0, 0, jnp.where(i == n_tiles + 1, n_ff - 1, f))

    blocks = 2 * tm * D_MODEL * 4 + 2 * D_MODEL * tf * 2
    scratch_bytes = 2 * tm * D_MODEL * (2 + 4)
    temporaries = 2 * tm * tf * 4 + tm * D_MODEL * 4
    params = pltpu.CompilerParams(
        dimension_semantics=("arbitrary", "arbitrary"),
        vmem_limit_bytes=int(min(2 * blocks + scratch_bytes + temporaries + (6 << 20), V7X_VMEM_BYTES - (4 << 20))))
    return pl.pallas_call(
        functools.partial(_mlp_kernel, n_tiles=n_tiles),
        out_shape=jax.ShapeDtypeStruct((m, D_MODEL), F32),
        grid=(n_tiles + 2, n_ff),
        in_specs=[
            pl.BlockSpec((tm, D_MODEL), lambda i, f: (jnp.minimum(i, n_tiles - 1), 0)),
            pl.BlockSpec((1, D_MODEL), lambda i, f: (0, 0)),
            pl.BlockSpec((D_MODEL, tf), lambda i, f: (0, ff_blk(i, f))),
            pl.BlockSpec((tf, D_MODEL), lambda i, f: (ff_blk(i, f), 0)),
            pl.BlockSpec((1, D_MODEL), lambda i, f: (0, 0)),
        ],
        out_specs=pl.BlockSpec((tm, D_MODEL), lambda i, f: (jnp.clip(i - 2, 0, n_tiles - 1), 0)),
        scratch_shapes=[pltpu.VMEM((tm, D_MODEL), BF16), pltpu.VMEM((tm, D_MODEL), BF16),
                        pltpu.VMEM((tm, D_MODEL), F32), pltpu.VMEM((tm, D_MODEL), F32)],
        compiler_params=params,
        name="mlp_final_norm",
    )(h2d, norm_w.reshape(1, D_MODEL), w_up, w_down, final_w.reshape(1, D_MODEL))


def _encode(x, p):
    b, seq, d = x.shape
    x2d = x.reshape(b * seq, d)
    z2d = _inproj(x2d, p["norm_mix_w"], p["w_in"])
    z3 = z2d.reshape(b, seq, IN_COLS)
    a = _neighbourhood_attention(z3, p["na_bias"])
    r = _retention(z3, p["rope"], p["ret_tables"])
    m2d = _branch_mix(a.reshape(b * seq, NA_WIDTH), r.reshape(b * seq, RET_V_WIDTH), z2d,
                      p["w_na_out"], p["w_ret_out"])
    h2d = _outproj(x2d, m2d, p["w_out"])
    y2d = _mlp(h2d, p["norm_mlp_w"], p["w_mlp_up"], p["w_mlp_down"], p["norm_final_w"])
    return y2d.reshape(b, seq, d)


def kernel(x_prompt, x_sample, norm_mix_w, w_in, na_rpb, ret_decay_fwd, ret_decay_bwd, w_na_out, w_ret_out,
           w_out, norm_mlp_w, w_mlp_up, w_mlp_down, norm_final_w):
    depth = w_in.shape[0]
    seq = x_prompt.shape[1]
    assert x_sample.shape[1] == seq and seq % RET_TB == 0 and seq % (NA_SUB * NA_TQ) == 0
    assert depth == 1, "the final norm is fused into the last layer's MLP kernel"
    rope = _rope_tables(seq)
    l = 0
    params = {
        "norm_mix_w": norm_mix_w[l].astype(F32),
        "w_in": w_in[l].astype(BF16),
        "na_bias": _na_column_table(na_rpb[l]),
        "rope": rope,
        "ret_tables": _retention_tables(ret_decay_fwd[l], ret_decay_bwd[l]),
        "w_na_out": w_na_out[l].astype(BF16),
        "w_ret_out": w_ret_out[l].astype(BF16),
        "w_out": w_out[l].astype(BF16),
        "norm_mlp_w": norm_mlp_w[l].astype(F32),
        "w_mlp_up": w_mlp_up[l].astype(BF16),
        "w_mlp_down": w_mlp_down[l].astype(BF16),
        "norm_final_w": norm_final_w.astype(F32),
    }
    return _encode(x_prompt, params), _encode(x_sample, params)
```

```python
import functools

import jax
import jax.numpy as jnp
import numpy as np
from jax import lax
from jax.experimental import pallas as pl
from jax.experimental.pallas import tpu as pltpu

F32 = jnp.float32
BF16 = jnp.bfloat16

D_MODEL = 2048
GRID_W = 64
NA_HEADS = 8
NA_HEAD_DIM = 128
NA_WIDTH = NA_HEADS * NA_HEAD_DIM
NA_WIN_ROWS = 8
NA_WIN_COLS = 16
RET_HEADS = 8
RET_QK_DIM = 128
RET_V_DIM = 256
RET_QK_WIDTH = RET_HEADS * RET_QK_DIM
RET_V_WIDTH = RET_HEADS * RET_V_DIM
ROPE_BASE = 10000.0
D_FF = 4 * D_MODEL
EPS = 1e-6

OFF_NA_Q = 0
OFF_NA_K = OFF_NA_Q + NA_WIDTH
OFF_NA_V = OFF_NA_K + NA_WIDTH
OFF_RET_Q = OFF_NA_V + NA_WIDTH
OFF_RET_K = OFF_RET_Q + RET_QK_WIDTH
OFF_RET_V = OFF_RET_K + RET_QK_WIDTH
OFF_RET_G = OFF_RET_V + RET_V_WIDTH
OFF_GATE_A = OFF_RET_G + RET_V_WIDTH
OFF_GATE_B = OFF_GATE_A + D_MODEL
IN_COLS = OFF_GATE_B + D_MODEL

MASK_VALUE = -1e30
V7X_VMEM_BYTES = 64 * 1024 * 1024


def _compiler_params(semantics, block_bytes):
    limit = min(2 * block_bytes + (8 << 20), V7X_VMEM_BYTES - (4 << 20))
    return pltpu.CompilerParams(dimension_semantics=semantics, vmem_limit_bytes=int(limit))


IN_TM = 1024
IN_TN = 1024


IN_CAST_STEPS = 64


def _inproj_kernel(x_ref, nw_ref, w_ref, *refs, n_cast):
    cast_in, z_ref, cast_out, xn_ref = refs[:n_cast], refs[n_cast], refs[n_cast + 1:-1], refs[-1]

    @pl.when(pl.program_id(1) == 0)
    def _():
        x = x_ref[...]
        ms = jnp.mean(x * x, axis=-1, keepdims=True)
        xn_ref[...] = (x * lax.rsqrt(ms + EPS) * nw_ref[...]).astype(BF16)

    z_ref[...] = jnp.dot(xn_ref[...], w_ref[...], preferred_element_type=F32).astype(z_ref.dtype)
    for src_ref, dst_ref in zip(cast_in, cast_out):
        dst_ref[...] = src_ref[...].astype(dst_ref.dtype)


def _inproj(x2d, norm_w, w_in, cast=()):
    m = x2d.shape[0]
    tm, tn = IN_TM, IN_TN
    n_j = IN_COLS // tn
    assert not cast or (m // tm) * n_j >= IN_CAST_STEPS

    def slab(i, j):
        return jnp.minimum(i * n_j + j, IN_CAST_STEPS - 1), 0

    cast_specs = [pl.BlockSpec((w.shape[0] // IN_CAST_STEPS, w.shape[1]), slab) for w in cast]
    cast_bytes = sum(w.size // IN_CAST_STEPS * (4 + 2) for w in cast)
    blocks = tm * D_MODEL * 4 + D_MODEL * tn * 2 + tm * tn * 2 + tm * D_MODEL * 2 + cast_bytes
    z, *casted = pl.pallas_call(
        functools.partial(_inproj_kernel, n_cast=len(cast)),
        out_shape=[jax.ShapeDtypeStruct((m, IN_COLS), BF16)]
        + [jax.ShapeDtypeStruct(w.shape, BF16) for w in cast],
        grid=(m // tm, n_j),
        in_specs=[
            pl.BlockSpec((tm, D_MODEL), lambda i, j: (i, 0)),
            pl.BlockSpec((1, D_MODEL), lambda i, j: (0, 0)),
            pl.BlockSpec((D_MODEL, tn), lambda i, j: (0, j)),
        ] + cast_specs,
        out_specs=[pl.BlockSpec((tm, tn), lambda i, j: (i, j))] + cast_specs,
        scratch_shapes=[pltpu.VMEM((tm, D_MODEL), BF16)],
        compiler_params=_compiler_params(("arbitrary" if cast else "parallel", "arbitrary"), blocks),
        name="norm_inproj",
    )(x2d, norm_w.reshape(1, D_MODEL), w_in, *cast)
    return z, tuple(casted)


NA_SUB = 8
NA_Q_ROWS = 8
LOG2_E = 1.4426950408889634
NA_Q_SCALE = NA_HEAD_DIM ** -0.5 * LOG2_E
NA_K_ROWS = 16
NA_TQ = NA_Q_ROWS * GRID_W
NA_TK = NA_K_ROWS * GRID_W
NA_CQ = 16
NA_CK = 32
NA_N_CHUNKS = GRID_W // NA_CQ
NA_CHUNK_Q = NA_Q_ROWS * NA_CQ
NA_CHUNK_K = NA_K_ROWS * NA_CK
SUBLANES = 8


def _na_key_col_starts():
    starts = []
    for cq in range(NA_N_CHUNKS):
        c = np.arange(cq * NA_CQ, (cq + 1) * NA_CQ)
        c0 = np.clip(c - NA_WIN_COLS // 2, 0, GRID_W - NA_WIN_COLS)
        start = min((int(c0.min()) // SUBLANES) * SUBLANES, GRID_W - NA_CK)
        assert start <= c0.min() and c0.max() + NA_WIN_COLS <= start + NA_CK
        starts.append(start)
    return tuple(starts)


NA_KEY_COL_START = _na_key_col_starts()


NA_INVALID_SLOT = 2 * NA_WIN_ROWS - 1
NA_LANES = 128
NA_LANE_REP = NA_LANES // NA_CK


def _na_row_slots(rows):
    ri = np.arange(NA_Q_ROWS)
    kri = np.arange(NA_K_ROWS)
    half = NA_WIN_ROWS // 2
    n_blocks = rows // NA_Q_ROWS
    row_idx = []
    for kind in range(3):
        q_base = {0: 0, 1: NA_Q_ROWS, 2: (n_blocks - 1) * NA_Q_ROWS}[kind]
        k_base = int(np.clip(q_base - half, 0, rows - NA_K_ROWS))
        r = q_base + ri
        r0 = np.clip(r - half, 0, rows - NA_WIN_ROWS)
        kr = k_base + kri
        valid = (kr[None, :] >= r0[:, None]) & (kr[None, :] < r0[:, None] + NA_WIN_ROWS)
        idx = kr[None, :] - r[:, None] + NA_WIN_ROWS - 1
        row_idx.append(np.where(valid, idx, NA_INVALID_SLOT))
    return np.stack(row_idx)


def _na_column_table(rpb):
    cl = np.arange(NA_CQ)
    kcl = np.arange(NA_CK)
    c = (np.arange(NA_N_CHUNKS) * NA_CQ)[:, None, None] + cl[None, :, None]
    kc = np.asarray(NA_KEY_COL_START)[:, None, None] + kcl[None, None, :]
    c0 = np.clip(c - NA_WIN_COLS // 2, 0, GRID_W - NA_WIN_COLS)
    col_valid = (kc >= c0) & (kc < c0 + NA_WIN_COLS)
    col_idx = np.clip(kc - c + NA_WIN_COLS - 1, 0, 2 * NA_WIN_COLS - 2)
    bias = rpb.astype(F32)[:, :, col_idx] * LOG2_E
    tc = jnp.where(col_valid[None, None], bias, MASK_VALUE)
    tc = jnp.concatenate([tc, jnp.full((NA_HEADS, 1) + tc.shape[2:], MASK_VALUE, F32)], axis=1)
    tc = jnp.transpose(tc, (0, 2, 1, 3, 4))
    return jnp.tile(tc, (1, 1, 1, 1, NA_LANE_REP))


def _na_build_bias(tc_ref, bias_ref, row_slots):
    lane = lax.broadcasted_iota(jnp.int32, (NA_CQ, NA_LANES), 1)
    for kind in range(3):
        for cq in range(NA_N_CHUNKS):
            for ri in range(NA_Q_ROWS):
                for grp in range(NA_K_ROWS // NA_LANE_REP):
                    slots = [int(row_slots[kind, ri, grp * NA_LANE_REP + t]) for t in range(NA_LANE_REP)]
                    tile = tc_ref[cq, slots[-1]]
                    for t in range(NA_LANE_REP - 2, -1, -1):
                        if slots[t] != slots[t + 1]:
                            tile = jnp.where(lane < (t + 1) * NA_CK, tc_ref[cq, slots[t]], tile)
                    bias_ref[kind, cq, ri * NA_CQ:(ri + 1) * NA_CQ, grp * NA_LANES:(grp + 1) * NA_LANES] = tile


def _na_kernel(q_ref, k_ref, v_ref, tc_ref, o_ref, bias_ref, *, seq, row_slots):
    hd = NA_HEAD_DIM
    half_rows = (NA_WIN_ROWS // 2) * GRID_W
    n_blocks = seq // NA_TQ

    @pl.when((pl.program_id(1) == 0) & (pl.program_id(2) == 0))
    def _():
        _na_build_bias(tc_ref, bias_ref, row_slots)

    pairs = [(sb, cq) for sb in range(NA_SUB) for cq in range(NA_N_CHUNKS)]
    qs, ks, vs, kinds = [], [], [], []
    for sb in range(NA_SUB):
        blk = pl.program_id(2) * NA_SUB + sb
        start = pl.multiple_of(jnp.clip(blk * NA_TQ - half_rows, 0, seq - NA_TK), half_rows)
        q = q_ref[sb * NA_TQ:(sb + 1) * NA_TQ, :].astype(F32) * NA_Q_SCALE
        qs.append(q.reshape(NA_Q_ROWS, GRID_W, hd))
        ks.append(k_ref[pl.ds(start, NA_TK), :].astype(F32).reshape(NA_K_ROWS, GRID_W, hd))
        vs.append(v_ref[pl.ds(start, NA_TK), :].astype(F32).reshape(NA_K_ROWS, GRID_W, hd))
        kinds.append(jnp.where(blk == 0, 0, jnp.where(blk == n_blocks - 1, 2, 1)))
    scores = []
    for sb, cq in pairs:
        k0 = NA_KEY_COL_START[cq]
        qc = qs[sb][:, cq * NA_CQ:(cq + 1) * NA_CQ, :].reshape(NA_CHUNK_Q, hd).astype(BF16)
        kc = ks[sb][:, k0:k0 + NA_CK, :].reshape(NA_CHUNK_K, hd).astype(BF16)
        scores.append(lax.dot_general(qc, kc, (((1,), (1,)), ((), ())), preferred_element_type=F32))
    probs, denoms = [], []
    for (sb, cq), s in zip(pairs, scores):
        s = s + bias_ref[kinds[sb], cq]
        p = jnp.exp2(s - jnp.max(s, axis=-1, keepdims=True))
        denoms.append(jnp.sum(p, axis=-1, keepdims=True))
        probs.append(p.astype(BF16))
    outs = []
    for (sb, cq), p, l in zip(pairs, probs, denoms):
        k0 = NA_KEY_COL_START[cq]
        vc = vs[sb][:, k0:k0 + NA_CK, :].reshape(NA_CHUNK_K, hd).astype(BF16)
        o = jnp.dot(p, vc, preferred_element_type=F32) / l
        outs.append(o.reshape(NA_Q_ROWS, NA_CQ, hd))
    for sb in range(NA_SUB):
        o = jnp.concatenate(outs[sb * NA_N_CHUNKS:(sb + 1) * NA_N_CHUNKS], axis=1)
        o_ref[sb * NA_TQ:(sb + 1) * NA_TQ, :] = o.reshape(NA_TQ, hd).astype(o_ref.dtype)


def _neighbourhood_attention(z3, col_table):
    b, seq, _ = z3.shape
    tq = NA_SUB * NA_TQ
    hd = NA_HEAD_DIM
    row_slots = _na_row_slots(seq // GRID_W)
    table_bytes = NA_N_CHUNKS * (NA_INVALID_SLOT + 1) * NA_CQ * NA_LANES * 4
    bias_bytes = 3 * NA_N_CHUNKS * NA_CHUNK_Q * NA_CHUNK_K * 4
    blocks = tq * hd * 2 * 2 + 2 * seq * hd * 2 + table_bytes
    temporaries = NA_SUB * (3 * NA_TK * hd * 4 + 3 * NA_N_CHUNKS * NA_CHUNK_Q * NA_CHUNK_K * 4)
    params = pltpu.CompilerParams(
        dimension_semantics=("arbitrary", "arbitrary", "arbitrary"),
        vmem_limit_bytes=int(2 * blocks + bias_bytes + temporaries + (8 << 20)))
    return pl.pallas_call(
        functools.partial(_na_kernel, seq=seq, row_slots=row_slots),
        out_shape=jax.ShapeDtypeStruct((b, seq, NA_WIDTH), BF16),
        grid=(NA_HEADS, b, seq // tq),
        in_specs=[
            pl.BlockSpec((None, tq, hd), lambda h, bi, s: (bi, s, OFF_NA_Q // hd + h)),
            pl.BlockSpec((None, seq, hd), lambda h, bi, s: (bi, 0, OFF_NA_K // hd + h)),
            pl.BlockSpec((None, seq, hd), lambda h, bi, s: (bi, 0, OFF_NA_V // hd + h)),
            pl.BlockSpec((None, NA_N_CHUNKS, NA_INVALID_SLOT + 1, NA_CQ, NA_LANES),
                         lambda h, bi, s: (h, 0, 0, 0, 0)),
        ],
        out_specs=pl.BlockSpec((None, tq, hd), lambda h, bi, s: (bi, s, h)),
        scratch_shapes=[pltpu.VMEM((3, NA_N_CHUNKS, NA_CHUNK_Q, NA_CHUNK_K), F32)],
        compiler_params=params,
        name="neighbourhood_attention",
    )(z3, z3, z3, col_table)


RET_CHUNK = 256
RET_TB = 2048
RET_NC = RET_TB // RET_CHUNK


def _rope_tables(seq):
    half = RET_QK_DIM // 2
    inv_freq = ROPE_BASE ** (-jnp.arange(half, dtype=F32) / half)
    ang = jnp.arange(seq, dtype=F32)[:, None] * inv_freq[None, :]
    cos, sin = jnp.cos(ang), jnp.sin(ang)
    return jnp.concatenate([cos, cos], axis=-1), jnp.concatenate([-sin, sin], axis=-1)


def _decay_tables(decay, backward):
    c = RET_CHUNK
    lg = jax.nn.log_sigmoid(decay.astype(F32))
    pos = jnp.arange(c, dtype=F32)
    diff = pos[:, None] - pos[None, :]
    lg3 = lg[:, None, None]
    if backward:
        decay_in = jnp.exp(jnp.where(diff < 0, -diff * lg3, -jnp.inf))
        q_dec = jnp.exp((c - pos)[None, :] * lg[:, None])
        k_dec = jnp.exp(pos[None, :] * lg[:, None])
    else:
        decay_in = jnp.exp(jnp.where(diff >= 0, diff * lg3, -jnp.inf))
        q_dec = jnp.exp((pos + 1.0)[None, :] * lg[:, None])
        k_dec = jnp.exp((c - 1.0 - pos)[None, :] * lg[:, None])
    chunk_dec = jnp.exp(c * lg)
    return decay_in, q_dec, k_dec, chunk_dec


def _retention_tables(decay_fwd, decay_bwd):
    c, dk = RET_CHUNK, RET_QK_DIM
    din_f, qd_f, kd_f, cd_f = _decay_tables(decay_fwd, False)
    din_b, qd_b, kd_b, cd_b = _decay_tables(decay_bwd, True)
    qd2 = jnp.concatenate([jnp.broadcast_to(qd_f[:, :, None], (RET_HEADS, c, dk)),
                           jnp.broadcast_to(qd_b[:, :, None], (RET_HEADS, c, dk))], axis=2)
    kdt2 = jnp.concatenate([jnp.broadcast_to(kd_f[:, None, :], (RET_HEADS, dk, c)),
                            jnp.broadcast_to(kd_b[:, None, :], (RET_HEADS, dk, c))], axis=1)
    return din_f + din_b, qd2, kdt2, jnp.stack([cd_f, cd_b])


def _retention_kernel(cd_ref, q_ref, k_ref, v_ref, cos_ref, sin_ref, din_ref, qd_ref, kdt_ref, g_ref,
                      o_ref, statef_ref, stateb_ref, kt_ref, kvf_ref, sb_ref, vc_ref, *, n_blk):
    h = pl.program_id(1)
    pass_id = pl.program_id(2)
    j = pl.program_id(3)
    dk, half = RET_QK_DIM, RET_QK_DIM // 2
    chunks = range(RET_NC)
    slices = [slice(c * RET_CHUNK, (c + 1) * RET_CHUNK) for c in chunks]

    @pl.when(j == 0)
    def _():
        statef_ref[...] = jnp.zeros_like(statef_ref)
        stateb_ref[...] = jnp.zeros_like(stateb_ref)

    def block_rows(blk):
        return pl.ds(pl.multiple_of(blk * RET_TB, RET_TB), RET_TB)

    def rotate(ref, blk):
        x = ref[...].astype(F32)
        return x * cos_ref[block_rows(blk), :] + pltpu.roll(x, half, 1) * sin_ref[block_rows(blk), :]

    @pl.when(pass_id == 0)
    def _():
        blk = n_blk - 1 - j
        kt = (rotate(k_ref, blk) * (dk ** -0.5)).T
        kt_ref[blk] = kt.astype(BF16)
        vc_ref[block_rows(blk), :] = v_ref[...]
        kdt2 = kdt_ref[...]
        kv = [jnp.dot((jnp.concatenate([kt[:, sl], kt[:, sl]], axis=0) * kdt2).astype(BF16), v_ref[sl, :],
                      preferred_element_type=F32) for sl in slices]
        cd_b = cd_ref[1, h]
        state = stateb_ref[...]
        for c in reversed(chunks):
            sb_ref[blk * RET_NC + c] = state.astype(BF16)
            kvf_ref[blk * RET_NC + c] = kv[c][:dk]
            state = cd_b * state + kv[c][dk:]
        stateb_ref[...] = state

    @pl.when(pass_id == 1)
    def _():
        blk = j
        q = rotate(q_ref, blk)
        kt = kt_ref[blk]
        din, qd2 = din_ref[...], qd_ref[...]
        scores = [jnp.dot(q[sl].astype(BF16), kt[:, sl], preferred_element_type=F32) for sl in slices]
        decayed = [(s * din).astype(BF16) for s in scores]
        cd_f = cd_ref[0, h]
        states = [statef_ref[...]]
        for c in chunks:
            states.append(cd_f * states[-1] + kvf_ref[blk * RET_NC + c])
        statef_ref[...] = states[-1]
        for c, sl in zip(chunks, slices):
            lhs = jnp.concatenate(
                [decayed[c], (jnp.concatenate([q[sl], q[sl]], axis=1) * qd2).astype(BF16)], axis=1)
            v_rows = pl.ds(pl.multiple_of(blk * RET_TB + c * RET_CHUNK, RET_CHUNK), RET_CHUNK)
            rhs = jnp.concatenate(
                [vc_ref[v_rows, :], states[c].astype(BF16), sb_ref[blk * RET_NC + c]], axis=0)
            o = jnp.dot(lhs, rhs, preferred_element_type=F32)
            o = o * lax.rsqrt(jnp.mean(o * o, axis=-1, keepdims=True) + EPS)
            g = g_ref[sl, :].astype(F32)
            o_ref[sl, :] = (o * (g * jax.nn.sigmoid(g))).astype(o_ref.dtype)


def _retention(z3, rope, tables):
    b, seq, _ = z3.shape
    n_blk = seq // RET_TB
    dk, dv, c = RET_QK_DIM, RET_V_DIM, RET_CHUNK
    cos2, sin2 = rope
    din, qd, kdt, cd = tables

    def pass1_blk(p, j):
        return jnp.where(p == 0, 0, j)

    def pass0_blk(p, j):
        return jnp.where(p == 0, n_blk - 1 - j, 0)

    n_chunks = seq // c
    blocks = RET_TB * (2 * dk * 2 + dv * 2 + 2 * dv * 2) + 2 * seq * dk * 4 + 3 * c * c * 4
    scratch_bytes = 2 * dk * dv * 4 + seq * dk * 2 + n_chunks * dk * dv * (4 + 2) + seq * dv * 2
    temporaries = 6 * RET_TB * dk * 4 + 16 * c * c * 4
    params = pltpu.CompilerParams(
        dimension_semantics=("parallel", "parallel", "arbitrary", "arbitrary"),
        vmem_limit_bytes=int(2 * blocks + scratch_bytes + temporaries + (8 << 20)))
    return pl.pallas_call(
        functools.partial(_retention_kernel, n_blk=n_blk),
        out_shape=jax.ShapeDtypeStruct((b, seq, RET_V_WIDTH), BF16),
        grid=(b, RET_HEADS, 2, n_blk),
        in_specs=[
            pl.BlockSpec(memory_space=pltpu.SMEM),
            pl.BlockSpec((None, RET_TB, dk), lambda bi, h, p, j: (bi, pass1_blk(p, j), OFF_RET_Q // dk + h)),
            pl.BlockSpec((None, RET_TB, dk), lambda bi, h, p, j: (bi, pass0_blk(p, j), OFF_RET_K // dk + h)),
            pl.BlockSpec((None, RET_TB, dv), lambda bi, h, p, j: (bi, pass0_blk(p, j), OFF_RET_V // dv + h)),
            pl.BlockSpec((seq, dk), lambda bi, h, p, j: (0, 0)),
            pl.BlockSpec((seq, dk), lambda bi, h, p, j: (0, 0)),
            pl.BlockSpec((None, c, c), lambda bi, h, p, j: (h, 0, 0)),
            pl.BlockSpec((None, c, 2 * dk), lambda bi, h, p, j: (h, 0, 0)),
            pl.BlockSpec((None, 2 * dk, c), lambda bi, h, p, j: (h, 0, 0)),
            pl.BlockSpec((None, RET_TB, dv), lambda bi, h, p, j: (bi, pass1_blk(p, j), OFF_RET_G // dv + h)),
        ],
        out_specs=pl.BlockSpec((None, RET_TB, dv), lambda bi, h, p, j: (bi, pass1_blk(p, j), h)),
        scratch_shapes=[
            pltpu.VMEM((dk, dv), F32),
            pltpu.VMEM((dk, dv), F32),
            pltpu.VMEM((n_blk, dk, RET_TB), BF16),
            pltpu.VMEM((n_chunks, dk, dv), F32),
            pltpu.VMEM((n_chunks, dk, dv), BF16),
            pltpu.VMEM((seq, dv), BF16),
        ],
        compiler_params=params,
        name="retention",
    )(cd, z3, z3, z3, cos2, sin2, din, qd, kdt, z3)


MIX_TM = 1024
MIX_TN = 1024


def _mix_kernel(a_ref, r_ref, wna_ref, wret_ref, ga_ref, gb_ref, m_ref):
    a = jnp.dot(a_ref[...], wna_ref[...], preferred_element_type=F32)
    r = jnp.dot(r_ref[...], wret_ref[...], preferred_element_type=F32)
    ga = jax.nn.sigmoid(ga_ref[...].astype(F32))
    gb = jax.nn.sigmoid(gb_ref[...].astype(F32))
    m_ref[...] = (ga * a + gb * r).astype(m_ref.dtype)


def _branch_mix(a2d, r2d, z2d, w_na_out, w_ret_out):
    m = a2d.shape[0]
    tm, tn = MIX_TM, MIX_TN
    blocks = tm * NA_WIDTH * 2 + tm * RET_V_WIDTH * 2 + (NA_WIDTH + RET_V_WIDTH) * tn * 2 + 3 * tm * tn * 2
    blocks += 4 * tm * tn * 4
    return pl.pallas_call(
        _mix_kernel,
        out_shape=jax.ShapeDtypeStruct((m, D_MODEL), BF16),
        grid=(m // tm, D_MODEL // tn),
        in_specs=[
            pl.BlockSpec((tm, NA_WIDTH), lambda i, j: (i, 0)),
            pl.BlockSpec((tm, RET_V_WIDTH), lambda i, j: (i, 0)),
            pl.BlockSpec((NA_WIDTH, tn), lambda i, j: (0, j)),
            pl.BlockSpec((RET_V_WIDTH, tn), lambda i, j: (0, j)),
            pl.BlockSpec((tm, tn), lambda i, j: (i, OFF_GATE_A // tn + j)),
            pl.BlockSpec((tm, tn), lambda i, j: (i, OFF_GATE_B // tn + j)),
        ],
        out_specs=pl.BlockSpec((tm, tn), lambda i, j: (i, j)),
        compiler_params=_compiler_params(("parallel", "arbitrary"), blocks),
        name="branch_mix",
    )(a2d, r2d, w_na_out, w_ret_out, z2d, z2d)


OUT_TM = 512
OUT_TN = D_MODEL


def _outproj_kernel(x_ref, m_ref, w_ref, h_ref):
    h_ref[...] = x_ref[...] + jnp.dot(m_ref[...], w_ref[...], preferred_element_type=F32)


def _outproj(x2d, m2d, w_out):
    m = x2d.shape[0]
    tm, tn = OUT_TM, OUT_TN
    blocks = 2 * tm * tn * 4 + tm * D_MODEL * 2 + D_MODEL * tn * 2 + tm * tn * 4
    return pl.pallas_call(
        _outproj_kernel,
        out_shape=jax.ShapeDtypeStruct((m, D_MODEL), F32),
        grid=(m // tm, D_MODEL // tn),
        in_specs=[
            pl.BlockSpec((tm, tn), lambda i, j: (i, j)),
            pl.BlockSpec((tm, D_MODEL), lambda i, j: (i, 0)),
            pl.BlockSpec((D_MODEL, tn), lambda i, j: (0, j)),
        ],
        out_specs=pl.BlockSpec((tm, tn), lambda i, j: (i, j)),
        compiler_params=_compiler_params(("parallel", "arbitrary"), blocks),
        name="outproj_residual",
    )(x2d, m2d, w_out)


MLP_TM = 512
MLP_TF = 1024


def _mlp_kernel(h_ref, nw_ref, wup_ref, wdown_ref, fw_ref, y_ref, hn_ref, acc_ref):
    f = pl.program_id(1)

    @pl.when(f == 0)
    def _():
        h = h_ref[...]
        ms = jnp.mean(h * h, axis=-1, keepdims=True)
        hn_ref[...] = (h * lax.rsqrt(ms + EPS) * nw_ref[...]).astype(BF16)
        acc_ref[...] = h

    u = jnp.maximum(jnp.dot(hn_ref[...], wup_ref[...], preferred_element_type=F32), 0.0)
    acc_ref[...] += jnp.dot((u * u).astype(BF16), wdown_ref[...], preferred_element_type=F32)

    @pl.when(f == pl.num_programs(1) - 1)
    def _():
        x = acc_ref[...]
        ms = jnp.mean(x * x, axis=-1, keepdims=True)
        y_ref[...] = x * lax.rsqrt(ms + EPS) * fw_ref[...]


def _mlp(h2d, norm_w, w_up, w_down, final_w):
    m = h2d.shape[0]
    tm, tf = MLP_TM, MLP_TF
    blocks = 2 * tm * D_MODEL * 4 + 2 * D_MODEL * tf * 2 + tm * D_MODEL * 2 + tm * D_MODEL * 4 + 2 * tm * tf * 4
    return pl.pallas_call(
        _mlp_kernel,
        out_shape=jax.ShapeDtypeStruct((m, D_MODEL), F32),
        grid=(m // tm, D_FF // tf),
        in_specs=[
            pl.BlockSpec((tm, D_MODEL), lambda i, f: (i, 0)),
            pl.BlockSpec((1, D_MODEL), lambda i, f: (0, 0)),
            pl.BlockSpec((D_MODEL, tf), lambda i, f: (0, f)),
            pl.BlockSpec((tf, D_MODEL), lambda i, f: (f, 0)),
            pl.BlockSpec((1, D_MODEL), lambda i, f: (0, 0)),
        ],
        out_specs=pl.BlockSpec((tm, D_MODEL), lambda i, f: (i, 0)),
        scratch_shapes=[pltpu.VMEM((tm, D_MODEL), BF16), pltpu.VMEM((tm, D_MODEL), F32)],
        compiler_params=_compiler_params(("parallel", "arbitrary"), blocks),
        name="mlp_final_norm",
    )(h2d, norm_w.reshape(1, D_MODEL), w_up, w_down, final_w.reshape(1, D_MODEL))


def _encode(x, p):
    b, seq, d = x.shape
    x2d = x.reshape(b * seq, d)
    late = [k for k in ("w_mlp_up", "w_mlp_down") if p[k].dtype != BF16]
    z2d, casted = _inproj(x2d, p["norm_mix_w"], p["w_in"], cast=tuple(p[k] for k in late))
    p = {**p, **dict(zip(late, casted))}
    z3 = z2d.reshape(b, seq, IN_COLS)
    a = _neighbourhood_attention(z3, p["na_bias"])
    r = _retention(z3, p["rope"], p["ret_tables"])
    m2d = _branch_mix(a.reshape(b * seq, NA_WIDTH), r.reshape(b * seq, RET_V_WIDTH), z2d,
                      p["w_na_out"], p["w_ret_out"])
    h2d = _outproj(x2d, m2d, p["w_out"])
    y2d = _mlp(h2d, p["norm_mlp_w"], p["w_mlp_up"], p["w_mlp_down"], p["norm_final_w"])
    return y2d.reshape(b, seq, d), p


def kernel(x_prompt, x_sample, norm_mix_w, w_in, na_rpb, ret_decay_fwd, ret_decay_bwd, w_na_out, w_ret_out,
           w_out, norm_mlp_w, w_mlp_up, w_mlp_down, norm_final_w):
    depth = w_in.shape[0]
    seq = x_prompt.shape[1]
    assert x_sample.shape[1] == seq and seq % RET_TB == 0 and seq % (NA_SUB * NA_TQ) == 0
    assert depth == 1, "the final norm is fused into the last layer's MLP kernel"
    rope = _rope_tables(seq)
    l = 0
    params = {
        "norm_mix_w": norm_mix_w[l].astype(F32),
        "w_in": w_in[l].astype(BF16),
        "na_bias": _na_column_table(na_rpb[l]),
        "rope": rope,
        "ret_tables": _retention_tables(ret_decay_fwd[l], ret_decay_bwd[l]),
        "w_na_out": w_na_out[l].astype(BF16),
        "w_ret_out": w_ret_out[l].astype(BF16),
        "w_out": w_out[l].astype(BF16),
        "norm_mlp_w": norm_mlp_w[l].astype(F32),
        "w_mlp_up": w_mlp_up[l].astype(F32),
        "w_mlp_down": w_mlp_down[l].astype(F32),
        "norm_final_w": norm_final_w.astype(F32),
    }
    y_prompt, params = _encode(x_prompt, params)
    y_sample, _ = _encode(x_sample, params)
    return y_prompt, y_sample
```

```python
import functools

import jax
import jax.numpy as jnp
import numpy as np
from jax import lax
from jax.experimental import pallas as pl
from jax.experimental.pallas import tpu as pltpu

F32 = jnp.float32
BF16 = jnp.bfloat16

D_MODEL = 2048
GRID_W = 64
NA_HEADS = 8
NA_HEAD_DIM = 128
NA_WIDTH = NA_HEADS * NA_HEAD_DIM
NA_WIN_ROWS = 8
NA_WIN_COLS = 16
RET_HEADS = 8
RET_QK_DIM = 128
RET_V_DIM = 256
RET_QK_WIDTH = RET_HEADS * RET_QK_DIM
RET_V_WIDTH = RET_HEADS * RET_V_DIM
ROPE_BASE = 10000.0
D_FF = 4 * D_MODEL
EPS = 1e-6

OFF_NA_Q = 0
OFF_NA_K = OFF_NA_Q + NA_WIDTH
OFF_NA_V = OFF_NA_K + NA_WIDTH
OFF_RET_Q = OFF_NA_V + NA_WIDTH
OFF_RET_K = OFF_RET_Q + RET_QK_WIDTH
OFF_RET_V = OFF_RET_K + RET_QK_WIDTH
OFF_RET_G = OFF_RET_V + RET_V_WIDTH
OFF_GATE_A = OFF_RET_G + RET_V_WIDTH
OFF_GATE_B = OFF_GATE_A + D_MODEL
IN_COLS = OFF_GATE_B + D_MODEL

MASK_VALUE = -1e30
V7X_VMEM_BYTES = 64 * 1024 * 1024


def _compiler_params(semantics, block_bytes):
    limit = min(2 * block_bytes + (8 << 20), V7X_VMEM_BYTES - (4 << 20))
    return pltpu.CompilerParams(dimension_semantics=semantics, vmem_limit_bytes=int(limit))


IN_TM = 1024
IN_TN = 1024


IN_CAST_STEPS = 64


def _inproj_kernel(x_ref, nw_ref, w_ref, *refs, n_cast):
    cast_in, z_ref, cast_out, xn_ref = refs[:n_cast], refs[n_cast], refs[n_cast + 1:-1], refs[-1]

    @pl.when(pl.program_id(1) == 0)
    def _():
        x = x_ref[...]
        ms = jnp.mean(x * x, axis=-1, keepdims=True)
        xn_ref[...] = (x * lax.rsqrt(ms + EPS) * nw_ref[...]).astype(BF16)

    z_ref[...] = jnp.dot(xn_ref[...], w_ref[...], preferred_element_type=F32).astype(z_ref.dtype)
    for src_ref, dst_ref in zip(cast_in, cast_out):
        dst_ref[...] = src_ref[...].astype(dst_ref.dtype)


def _inproj(x2d, norm_w, w_in, cast=()):
    m = x2d.shape[0]
    tm, tn = IN_TM, IN_TN
    n_j = IN_COLS // tn
    assert not cast or (m // tm) * n_j >= IN_CAST_STEPS

    def slab(i, j):
        return jnp.minimum(i * n_j + j, IN_CAST_STEPS - 1), 0

    cast_specs = [pl.BlockSpec((w.shape[0] // IN_CAST_STEPS, w.shape[1]), slab) for w in cast]
    cast_bytes = sum(w.size // IN_CAST_STEPS * (4 + 2) for w in cast)
    blocks = tm * D_MODEL * 4 + D_MODEL * tn * 2 + tm * tn * 2 + tm * D_MODEL * 2 + cast_bytes
    z, *casted = pl.pallas_call(
        functools.partial(_inproj_kernel, n_cast=len(cast)),
        out_shape=[jax.ShapeDtypeStruct((m, IN_COLS), BF16)]
        + [jax.ShapeDtypeStruct(w.shape, BF16) for w in cast],
        grid=(m // tm, n_j),
        in_specs=[
            pl.BlockSpec((tm, D_MODEL), lambda i, j: (i, 0)),
            pl.BlockSpec((1, D_MODEL), lambda i, j: (0, 0)),
            pl.BlockSpec((D_MODEL, tn), lambda i, j: (0, j)),
        ] + cast_specs,
        out_specs=[pl.BlockSpec((tm, tn), lambda i, j: (i, j))] + cast_specs,
        scratch_shapes=[pltpu.VMEM((tm, D_MODEL), BF16)],
        compiler_params=_compiler_params(("arbitrary" if cast else "parallel", "arbitrary"), blocks),
        name="norm_inproj",
    )(x2d, norm_w.reshape(1, D_MODEL), w_in, *cast)
    return z, tuple(casted)


NA_SUB = 8
NA_Q_ROWS = 8
LOG2_E = 1.4426950408889634
NA_Q_SCALE = NA_HEAD_DIM ** -0.5 * LOG2_E
NA_K_ROWS = 16
NA_TQ = NA_Q_ROWS * GRID_W
NA_TK = NA_K_ROWS * GRID_W
NA_CQ = 16
NA_CK = 32
NA_N_CHUNKS = GRID_W // NA_CQ
NA_CHUNK_Q = NA_Q_ROWS * NA_CQ
NA_CHUNK_K = NA_K_ROWS * NA_CK
SUBLANES = 8


def _na_key_col_starts():
    starts = []
    for cq in range(NA_N_CHUNKS):
        c = np.arange(cq * NA_CQ, (cq + 1) * NA_CQ)
        c0 = np.clip(c - NA_WIN_COLS // 2, 0, GRID_W - NA_WIN_COLS)
        start = min((int(c0.min()) // SUBLANES) * SUBLANES, GRID_W - NA_CK)
        assert start <= c0.min() and c0.max() + NA_WIN_COLS <= start + NA_CK
        starts.append(start)
    return tuple(starts)


NA_KEY_COL_START = _na_key_col_starts()


NA_INVALID_SLOT = 2 * NA_WIN_ROWS - 1
NA_LANES = 128
NA_LANE_REP = NA_LANES // NA_CK


def _na_row_slots(rows):
    ri = np.arange(NA_Q_ROWS)
    kri = np.arange(NA_K_ROWS)
    half = NA_WIN_ROWS // 2
    n_blocks = rows // NA_Q_ROWS
    row_idx = []
    for kind in range(3):
        q_base = {0: 0, 1: NA_Q_ROWS, 2: (n_blocks - 1) * NA_Q_ROWS}[kind]
        k_base = int(np.clip(q_base - half, 0, rows - NA_K_ROWS))
        r = q_base + ri
        r0 = np.clip(r - half, 0, rows - NA_WIN_ROWS)
        kr = k_base + kri
        valid = (kr[None, :] >= r0[:, None]) & (kr[None, :] < r0[:, None] + NA_WIN_ROWS)
        idx = kr[None, :] - r[:, None] + NA_WIN_ROWS - 1
        row_idx.append(np.where(valid, idx, NA_INVALID_SLOT))
    return np.stack(row_idx)


def _na_column_table(rpb):
    cl = np.arange(NA_CQ)
    kcl = np.arange(NA_CK)
    c = (np.arange(NA_N_CHUNKS) * NA_CQ)[:, None, None] + cl[None, :, None]
    kc = np.asarray(NA_KEY_COL_START)[:, None, None] + kcl[None, None, :]
    c0 = np.clip(c - NA_WIN_COLS // 2, 0, GRID_W - NA_WIN_COLS)
    col_valid = (kc >= c0) & (kc < c0 + NA_WIN_COLS)
    col_idx = np.clip(kc - c + NA_WIN_COLS - 1, 0, 2 * NA_WIN_COLS - 2)
    bias = rpb.astype(F32)[:, :, col_idx] * LOG2_E
    tc = jnp.where(col_valid[None, None], bias, MASK_VALUE)
    tc = jnp.concatenate([tc, jnp.full((NA_HEADS, 1) + tc.shape[2:], MASK_VALUE, F32)], axis=1)
    tc = jnp.transpose(tc, (0, 2, 1, 3, 4))
    return jnp.tile(tc, (1, 1, 1, 1, NA_LANE_REP))


def _na_build_bias(tc_ref, bias_ref, row_slots):
    lane = lax.broadcasted_iota(jnp.int32, (NA_CQ, NA_LANES), 1)
    for kind in range(3):
        for cq in range(NA_N_CHUNKS):
            for ri in range(NA_Q_ROWS):
                for grp in range(NA_K_ROWS // NA_LANE_REP):
                    slots = [int(row_slots[kind, ri, grp * NA_LANE_REP + t]) for t in range(NA_LANE_REP)]
                    tile = tc_ref[cq, slots[-1]]
                    for t in range(NA_LANE_REP - 2, -1, -1):
                        if slots[t] != slots[t + 1]:
                            tile = jnp.where(lane < (t + 1) * NA_CK, tc_ref[cq, slots[t]], tile)
                    bias_ref[kind, cq, ri * NA_CQ:(ri + 1) * NA_CQ, grp * NA_LANES:(grp + 1) * NA_LANES] = tile


def _na_kernel(q_ref, k_ref, v_ref, tc_ref, o_ref, bias_ref, *, seq, row_slots):
    hd = NA_HEAD_DIM
    half_rows = (NA_WIN_ROWS // 2) * GRID_W
    n_blocks = seq // NA_TQ

    @pl.when((pl.program_id(1) == 0) & (pl.program_id(2) == 0))
    def _():
        _na_build_bias(tc_ref, bias_ref, row_slots)

    pairs = [(sb, cq) for sb in range(NA_SUB) for cq in range(NA_N_CHUNKS)]
    qs, ks, vs, kinds = [], [], [], []
    for sb in range(NA_SUB):
        blk = pl.program_id(2) * NA_SUB + sb
        start = pl.multiple_of(jnp.clip(blk * NA_TQ - half_rows, 0, seq - NA_TK), half_rows)
        q = q_ref[sb * NA_TQ:(sb + 1) * NA_TQ, :].astype(F32) * NA_Q_SCALE
        qs.append(q.reshape(NA_Q_ROWS, GRID_W, hd))
        ks.append(k_ref[pl.ds(start, NA_TK), :].astype(F32).reshape(NA_K_ROWS, GRID_W, hd))
        vs.append(v_ref[pl.ds(start, NA_TK), :].astype(F32).reshape(NA_K_ROWS, GRID_W, hd))
        kinds.append(jnp.where(blk == 0, 0, jnp.where(blk == n_blocks - 1, 2, 1)))
    scores = []
    for sb, cq in pairs:
        k0 = NA_KEY_COL_START[cq]
        qc = qs[sb][:, cq * NA_CQ:(cq + 1) * NA_CQ, :].reshape(NA_CHUNK_Q, hd).astype(BF16)
        kc = ks[sb][:, k0:k0 + NA_CK, :].reshape(NA_CHUNK_K, hd).astype(BF16)
        scores.append(lax.dot_general(qc, kc, (((1,), (1,)), ((), ())), preferred_element_type=F32))
    probs, denoms = [], []
    for (sb, cq), s in zip(pairs, scores):
        s = s + bias_ref[kinds[sb], cq]
        p = jnp.exp2(s - jnp.max(s, axis=-1, keepdims=True))
        denoms.append(jnp.sum(p, axis=-1, keepdims=True))
        probs.append(p.astype(BF16))
    outs = []
    for (sb, cq), p, l in zip(pairs, probs, denoms):
        k0 = NA_KEY_COL_START[cq]
        vc = vs[sb][:, k0:k0 + NA_CK, :].reshape(NA_CHUNK_K, hd).astype(BF16)
        o = jnp.dot(p, vc, preferred_element_type=F32) / l
        outs.append(o.reshape(NA_Q_ROWS, NA_CQ, hd))
    for sb in range(NA_SUB):
        o = jnp.concatenate(outs[sb * NA_N_CHUNKS:(sb + 1) * NA_N_CHUNKS], axis=1)
        o_ref[sb * NA_TQ:(sb + 1) * NA_TQ, :] = o.reshape(NA_TQ, hd).astype(o_ref.dtype)


def _neighbourhood_attention(z3, col_table):
    b, seq, _ = z3.shape
    tq = NA_SUB * NA_TQ
    hd = NA_HEAD_DIM
    row_slots = _na_row_slots(seq // GRID_W)
    table_bytes = NA_N_CHUNKS * (NA_INVALID_SLOT + 1) * NA_CQ * NA_LANES * 4
    bias_bytes = 3 * NA_N_CHUNKS * NA_CHUNK_Q * NA_CHUNK_K * 4
    blocks = tq * hd * 2 * 2 + 2 * seq * hd * 2 + table_bytes
    temporaries = NA_SUB * (3 * NA_TK * hd * 4 + 3 * NA_N_CHUNKS * NA_CHUNK_Q * NA_CHUNK_K * 4)
    params = pltpu.CompilerParams(
        dimension_semantics=("arbitrary", "arbitrary", "arbitrary"),
        vmem_limit_bytes=int(2 * blocks + bias_bytes + temporaries + (8 << 20)))
    return pl.pallas_call(
        functools.partial(_na_kernel, seq=seq, row_slots=row_slots),
        out_shape=jax.ShapeDtypeStruct((b, seq, NA_WIDTH), BF16),
        grid=(NA_HEADS, b, seq // tq),
        in_specs=[
            pl.BlockSpec((None, tq, hd), lambda h, bi, s: (bi, s, OFF_NA_Q // hd + h)),
            pl.BlockSpec((None, seq, hd), lambda h, bi, s: (bi, 0, OFF_NA_K // hd + h)),
            pl.BlockSpec((None, seq, hd), lambda h, bi, s: (bi, 0, OFF_NA_V // hd + h)),
            pl.BlockSpec((None, NA_N_CHUNKS, NA_INVALID_SLOT + 1, NA_CQ, NA_LANES),
                         lambda h, bi, s: (h, 0, 0, 0, 0)),
        ],
        out_specs=pl.BlockSpec((None, tq, hd), lambda h, bi, s: (bi, s, h)),
        scratch_shapes=[pltpu.VMEM((3, NA_N_CHUNKS, NA_CHUNK_Q, NA_CHUNK_K), F32)],
        compiler_params=params,
        name="neighbourhood_attention",
    )(z3, z3, z3, col_table)


RET_CHUNK = 256
RET_TB = 2048
RET_NC = RET_TB // RET_CHUNK


def _rope_tables(seq):
    half = RET_QK_DIM // 2
    inv_freq = ROPE_BASE ** (-jnp.arange(half, dtype=F32) / half)
    ang = jnp.arange(seq, dtype=F32)[:, None] * inv_freq[None, :]
    cos, sin = jnp.cos(ang), jnp.sin(ang)
    return jnp.concatenate([cos, cos], axis=-1), jnp.concatenate([-sin, sin], axis=-1)


def _decay_tables(decay, backward):
    c = RET_CHUNK
    lg = jax.nn.log_sigmoid(decay.astype(F32))
    pos = jnp.arange(c, dtype=F32)
    diff = pos[:, None] - pos[None, :]
    lg3 = lg[:, None, None]
    if backward:
        decay_in = jnp.exp(jnp.where(diff < 0, -diff * lg3, -jnp.inf))
        q_dec = jnp.exp((c - pos)[None, :] * lg[:, None])
        k_dec = jnp.exp(pos[None, :] * lg[:, None])
    else:
        decay_in = jnp.exp(jnp.where(diff >= 0, diff * lg3, -jnp.inf))
        q_dec = jnp.exp((pos + 1.0)[None, :] * lg[:, None])
        k_dec = jnp.exp((c - 1.0 - pos)[None, :] * lg[:, None])
    chunk_dec = jnp.exp(c * lg)
    return decay_in, q_dec, k_dec, chunk_dec


def _retention_tables(decay_fwd, decay_bwd):
    c, dk = RET_CHUNK, RET_QK_DIM
    din_f, qd_f, kd_f, cd_f = _decay_tables(decay_fwd, False)
    din_b, qd_b, kd_b, cd_b = _decay_tables(decay_bwd, True)
    qd2 = jnp.concatenate([jnp.broadcast_to(qd_f[:, :, None], (RET_HEADS, c, dk)),
                           jnp.broadcast_to(qd_b[:, :, None], (RET_HEADS, c, dk))], axis=2)
    kdt2 = jnp.concatenate([jnp.broadcast_to(kd_f[:, None, :], (RET_HEADS, dk, c)),
                            jnp.broadcast_to(kd_b[:, None, :], (RET_HEADS, dk, c))], axis=1)
    return din_f + din_b, qd2, kdt2, jnp.stack([cd_f, cd_b])


def _retention_kernel(cd_ref, q_ref, k_ref, v_ref, cos_ref, sin_ref, din_ref, qd_ref, kdt_ref, g_ref,
                      o_ref, statef_ref, stateb_ref, kt_ref, kvf_ref, sb_ref, vc_ref, *, n_blk):
    h = pl.program_id(1)
    pass_id = pl.program_id(2)
    j = pl.program_id(3)
    dk, half = RET_QK_DIM, RET_QK_DIM // 2
    chunks = range(RET_NC)
    slices = [slice(c * RET_CHUNK, (c + 1) * RET_CHUNK) for c in chunks]

    @pl.when(j == 0)
    def _():
        statef_ref[...] = jnp.zeros_like(statef_ref)
        stateb_ref[...] = jnp.zeros_like(stateb_ref)

    def block_rows(blk):
        return pl.ds(pl.multiple_of(blk * RET_TB, RET_TB), RET_TB)

    def rotate(ref, blk):
        x = ref[...].astype(F32)
        return x * cos_ref[block_rows(blk), :] + pltpu.roll(x, half, 1) * sin_ref[block_rows(blk), :]

    @pl.when(pass_id == 0)
    def _():
        blk = n_blk - 1 - j
        kt = (rotate(k_ref, blk) * (dk ** -0.5)).T
        kt_ref[blk] = kt.astype(BF16)
        vc_ref[block_rows(blk), :] = v_ref[...]
        kdt2 = kdt_ref[...]
        kv = [jnp.dot((jnp.concatenate([kt[:, sl], kt[:, sl]], axis=0) * kdt2).astype(BF16), v_ref[sl, :],
                      preferred_element_type=F32) for sl in slices]
        cd_b = cd_ref[1, h]
        state = stateb_ref[...]
        for c in reversed(chunks):
            sb_ref[blk * RET_NC + c] = state.astype(BF16)
            kvf_ref[blk * RET_NC + c] = kv[c][:dk]
            state = cd_b * state + kv[c][dk:]
        stateb_ref[...] = state

    @pl.when(pass_id == 1)
    def _():
        blk = j
        q = rotate(q_ref, blk)
        kt = kt_ref[blk]
        din, qd2 = din_ref[...], qd_ref[...]
        scores = [jnp.dot(q[sl].astype(BF16), kt[:, sl], preferred_element_type=F32) for sl in slices]
        decayed = [(s * din).astype(BF16) for s in scores]
        cd_f = cd_ref[0, h]
        states = [statef_ref[...]]
        for c in chunks:
            states.append(cd_f * states[-1] + kvf_ref[blk * RET_NC + c])
        statef_ref[...] = states[-1]
        for c, sl in zip(chunks, slices):
            lhs = jnp.concatenate(
                [decayed[c], (jnp.concatenate([q[sl], q[sl]], axis=1) * qd2).astype(BF16)], axis=1)
            v_rows = pl.ds(pl.multiple_of(blk * RET_TB + c * RET_CHUNK, RET_CHUNK), RET_CHUNK)
            rhs = jnp.concatenate(
                [vc_ref[v_rows, :], states[c].astype(BF16), sb_ref[blk * RET_NC + c]], axis=0)
            o = jnp.dot(lhs, rhs, preferred_element_type=F32)
            o = o * lax.rsqrt(jnp.mean(o * o, axis=-1, keepdims=True) + EPS)
            g = g_ref[sl, :]
            o_ref[sl, :] = o.astype(BF16) * (g * jax.nn.sigmoid(g))


def _retention(z3, rope, tables):
    b, seq, _ = z3.shape
    n_blk = seq // RET_TB
    dk, dv, c = RET_QK_DIM, RET_V_DIM, RET_CHUNK
    cos2, sin2 = rope
    din, qd, kdt, cd = tables

    def pass1_blk(p, j):
        return jnp.where(p == 0, 0, j)

    def pass0_blk(p, j):
        return jnp.where(p == 0, n_blk - 1 - j, 0)

    n_chunks = seq // c
    blocks = RET_TB * (2 * dk * 2 + dv * 2 + 2 * dv * 2) + 2 * seq * dk * 4 + 3 * c * c * 4
    scratch_bytes = 2 * dk * dv * 4 + seq * dk * 2 + n_chunks * dk * dv * (4 + 2) + seq * dv * 2
    temporaries = 6 * RET_TB * dk * 4 + 16 * c * c * 4
    params = pltpu.CompilerParams(
        dimension_semantics=("parallel", "parallel", "arbitrary", "arbitrary"),
        vmem_limit_bytes=int(2 * blocks + scratch_bytes + temporaries + (8 << 20)))
    return pl.pallas_call(
        functools.partial(_retention_kernel, n_blk=n_blk),
        out_shape=jax.ShapeDtypeStruct((b, seq, RET_V_WIDTH), BF16),
        grid=(b, RET_HEADS, 2, n_blk),
        in_specs=[
            pl.BlockSpec(memory_space=pltpu.SMEM),
            pl.BlockSpec((None, RET_TB, dk), lambda bi, h, p, j: (bi, pass1_blk(p, j), OFF_RET_Q // dk + h)),
            pl.BlockSpec((None, RET_TB, dk), lambda bi, h, p, j: (bi, pass0_blk(p, j), OFF_RET_K // dk + h)),
            pl.BlockSpec((None, RET_TB, dv), lambda bi, h, p, j: (bi, pass0_blk(p, j), OFF_RET_V // dv + h)),
            pl.BlockSpec((seq, dk), lambda bi, h, p, j: (0, 0)),
            pl.BlockSpec((seq, dk), lambda bi, h, p, j: (0, 0)),
            pl.BlockSpec((None, c, c), lambda bi, h, p, j: (h, 0, 0)),
            pl.BlockSpec((None, c, 2 * dk), lambda bi, h, p, j: (h, 0, 0)),
            pl.BlockSpec((None, 2 * dk, c), lambda bi, h, p, j: (h, 0, 0)),
            pl.BlockSpec((None, RET_TB, dv), lambda bi, h, p, j: (bi, pass1_blk(p, j), OFF_RET_G // dv + h)),
        ],
        out_specs=pl.BlockSpec((None, RET_TB, dv), lambda bi, h, p, j: (bi, pass1_blk(p, j), h)),
        scratch_shapes=[
            pltpu.VMEM((dk, dv), F32),
            pltpu.VMEM((dk, dv), F32),
            pltpu.VMEM((n_blk, dk, RET_TB), BF16),
            pltpu.VMEM((n_chunks, dk, dv), F32),
            pltpu.VMEM((n_chunks, dk, dv), BF16),
            pltpu.VMEM((seq, dv), BF16),
        ],
        compiler_params=params,
        name="retention",
    )(cd, z3, z3, z3, cos2, sin2, din, qd, kdt, z3)


MIX_TM = 1024
MIX_TN = 1024


def _mix_kernel(a_ref, r_ref, wna_ref, wret_ref, ga_ref, gb_ref, m_ref):
    a = jnp.dot(a_ref[...], wna_ref[...], preferred_element_type=F32)
    r = jnp.dot(r_ref[...], wret_ref[...], preferred_element_type=F32)
    ga = jax.nn.sigmoid(ga_ref[...].astype(F32))
    gb = jax.nn.sigmoid(gb_ref[...].astype(F32))
    m_ref[...] = (ga * a + gb * r).astype(m_ref.dtype)


def _branch_mix(a2d, r2d, z2d, w_na_out, w_ret_out):
    m = a2d.shape[0]
    tm, tn = MIX_TM, MIX_TN
    blocks = tm * NA_WIDTH * 2 + tm * RET_V_WIDTH * 2 + (NA_WIDTH + RET_V_WIDTH) * tn * 2 + 3 * tm * tn * 2
    blocks += 4 * tm * tn * 4
    return pl.pallas_call(
        _mix_kernel,
        out_shape=jax.ShapeDtypeStruct((m, D_MODEL), BF16),
        grid=(m // tm, D_MODEL // tn),
        in_specs=[
            pl.BlockSpec((tm, NA_WIDTH), lambda i, j: (i, 0)),
            pl.BlockSpec((tm, RET_V_WIDTH), lambda i, j: (i, 0)),
            pl.BlockSpec((NA_WIDTH, tn), lambda i, j: (0, j)),
            pl.BlockSpec((RET_V_WIDTH, tn), lambda i, j: (0, j)),
            pl.BlockSpec((tm, tn), lambda i, j: (i, OFF_GATE_A // tn + j)),
            pl.BlockSpec((tm, tn), lambda i, j: (i, OFF_GATE_B // tn + j)),
        ],
        out_specs=pl.BlockSpec((tm, tn), lambda i, j: (i, j)),
        compiler_params=_compiler_params(("parallel", "arbitrary"), blocks),
        name="branch_mix",
    )(a2d, r2d, w_na_out, w_ret_out, z2d, z2d)


OUT_TM = 512
OUT_TN = D_MODEL


def _outproj_kernel(x_ref, m_ref, w_ref, h_ref):
    h_ref[...] = x_ref[...] + jnp.dot(m_ref[...], w_ref[...], preferred_element_type=F32)


def _outproj(x2d, m2d, w_out):
    m = x2d.shape[0]
    tm, tn = OUT_TM, OUT_TN
    blocks = 2 * tm * tn * 4 + tm * D_MODEL * 2 + D_MODEL * tn * 2 + tm * tn * 4
    return pl.pallas_call(
        _outproj_kernel,
        out_shape=jax.ShapeDtypeStruct((m, D_MODEL), F32),
        grid=(m // tm, D_MODEL // tn),
        in_specs=[
            pl.BlockSpec((tm, tn), lambda i, j: (i, j)),
            pl.BlockSpec((tm, D_MODEL), lambda i, j: (i, 0)),
            pl.BlockSpec((D_MODEL, tn), lambda i, j: (0, j)),
        ],
        out_specs=pl.BlockSpec((tm, tn), lambda i, j: (i, j)),
        compiler_params=_compiler_params(("parallel", "arbitrary"), blocks),
        name="outproj_residual",
    )(x2d, m2d, w_out)


MLP_TM = 512
MLP_TF = 1024


def _mlp_kernel(h_ref, nw_ref, wup_ref, wdown_ref, fw_ref, y_ref, hn_ref, acc_ref):
    f = pl.program_id(1)

    @pl.when(f == 0)
    def _():
        h = h_ref[...]
        ms = jnp.mean(h * h, axis=-1, keepdims=True)
        hn_ref[...] = (h * lax.rsqrt(ms + EPS) * nw_ref[...]).astype(BF16)
        acc_ref[...] = h

    u = jnp.maximum(jnp.dot(hn_ref[...], wup_ref[...], preferred_element_type=F32), 0.0)
    acc_ref[...] += jnp.dot((u * u).astype(BF16), wdown_ref[...], preferred_element_type=F32)

    @pl.when(f == pl.num_programs(1) - 1)
    def _():
        x = acc_ref[...]
        ms = jnp.mean(x * x, axis=-1, keepdims=True)
        y_ref[...] = x * lax.rsqrt(ms + EPS) * fw_ref[...]


def _mlp(h2d, norm_w, w_up, w_down, final_w):
    m = h2d.shape[0]
    tm, tf = MLP_TM, MLP_TF
    blocks = 2 * tm * D_MODEL * 4 + 2 * D_MODEL * tf * 2 + tm * D_MODEL * 2 + tm * D_MODEL * 4 + 2 * tm * tf * 4
    return pl.pallas_call(
        _mlp_kernel,
        out_shape=jax.ShapeDtypeStruct((m, D_MODEL), F32),
        grid=(m // tm, D_FF // tf),
        in_specs=[
            pl.BlockSpec((tm, D_MODEL), lambda i, f: (i, 0)),
            pl.BlockSpec((1, D_MODEL), lambda i, f: (0, 0)),
            pl.BlockSpec((D_MODEL, tf), lambda i, f: (0, f)),
            pl.BlockSpec((tf, D_MODEL), lambda i, f: (f, 0)),
            pl.BlockSpec((1, D_MODEL), lambda i, f: (0, 0)),
        ],
        out_specs=pl.BlockSpec((tm, D_MODEL), lambda i, f: (i, 0)),
        scratch_shapes=[pltpu.VMEM((tm, D_MODEL), BF16), pltpu.VMEM((tm, D_MODEL), F32)],
        compiler_params=_compiler_params(("parallel", "arbitrary"), blocks),
        name="mlp_final_norm",
    )(h2d, norm_w.reshape(1, D_MODEL), w_up, w_down, final_w.reshape(1, D_MODEL))


LATE_WEIGHTS = ("w_na_out", "w_ret_out", "w_out", "w_mlp_up", "w_mlp_down")


def _encode(x, p):
    b, seq, d = x.shape
    x2d = x.reshape(b * seq, d)
    late = [k for k in LATE_WEIGHTS if p[k].dtype != BF16]
    z2d, casted = _inproj(x2d, p["norm_mix_w"], p["w_in"], cast=tuple(p[k] for k in late))
    p = {**p, **dict(zip(late, casted))}
    z3 = z2d.reshape(b, seq, IN_COLS)
    a = _neighbourhood_attention(z3, p["na_bias"])
    r = _retention(z3, p["rope"], p["ret_tables"])
    m2d = _branch_mix(a.reshape(b * seq, NA_WIDTH), r.reshape(b * seq, RET_V_WIDTH), z2d,
                      p["w_na_out"], p["w_ret_out"])
    h2d = _outproj(x2d, m2d, p["w_out"])
    y2d = _mlp(h2d, p["norm_mlp_w"], p["w_mlp_up"], p["w_mlp_down"], p["norm_final_w"])
    return y2d.reshape(b, seq, d), p


def kernel(x_prompt, x_sample, norm_mix_w, w_in, na_rpb, ret_decay_fwd, ret_decay_bwd, w_na_out, w_ret_out,
           w_out, norm_mlp_w, w_mlp_up, w_mlp_down, norm_final_w):
    depth = w_in.shape[0]
    seq = x_prompt.shape[1]
    assert x_sample.shape[1] == seq and seq % RET_TB == 0 and seq % (NA_SUB * NA_TQ) == 0
    assert depth == 1, "the final norm is fused into the last layer's MLP kernel"
    rope = _rope_tables(seq)
    l = 0
    params = {
        "norm_mix_w": norm_mix_w[l].astype(F32),
        "w_in": w_in[l].astype(BF16),
        "na_bias": _na_column_table(na_rpb[l]),
        "rope": rope,
        "ret_tables": _retention_tables(ret_decay_fwd[l], ret_decay_bwd[l]),
        "w_na_out": w_na_out[l].astype(F32),
        "w_ret_out": w_ret_out[l].astype(F32),
        "w_out": w_out[l].astype(F32),
        "norm_mlp_w": norm_mlp_w[l].astype(F32),
        "w_mlp_up": w_mlp_up[l].astype(F32),
        "w_mlp_down": w_mlp_down[l].astype(F32),
        "norm_final_w": norm_final_w.astype(F32),
    }
    y_prompt, params = _encode(x_prompt, params)
    y_sample, _ = _encode(x_sample, params)
    return y_prompt, y_sample
```

```python
import functools

import jax
import jax.numpy as jnp
import numpy as np
from jax import lax
from jax.experimental import pallas as pl
from jax.experimental.pallas import tpu as pltpu

F32 = jnp.float32
BF16 = jnp.bfloat16

D_MODEL = 2048
GRID_W = 64
NA_HEADS = 8
NA_HEAD_DIM = 128
NA_WIDTH = NA_HEADS * NA_HEAD_DIM
NA_WIN_ROWS = 8
NA_WIN_COLS = 16
RET_HEADS = 8
RET_QK_DIM = 128
RET_V_DIM = 256
RET_QK_WIDTH = RET_HEADS * RET_QK_DIM
RET_V_WIDTH = RET_HEADS * RET_V_DIM
ROPE_BASE = 10000.0
D_FF = 4 * D_MODEL
EPS = 1e-6

OFF_NA_Q = 0
OFF_NA_K = OFF_NA_Q + NA_WIDTH
OFF_NA_V = OFF_NA_K + NA_WIDTH
OFF_RET_Q = OFF_NA_V + NA_WIDTH
OFF_RET_K = OFF_RET_Q + RET_QK_WIDTH
OFF_RET_V = OFF_RET_K + RET_QK_WIDTH
OFF_RET_G = OFF_RET_V + RET_V_WIDTH
OFF_GATE_A = OFF_RET_G + RET_V_WIDTH
OFF_GATE_B = OFF_GATE_A + D_MODEL
IN_COLS = OFF_GATE_B + D_MODEL

MASK_VALUE = -1e30
V7X_VMEM_BYTES = 64 * 1024 * 1024


def _compiler_params(semantics, block_bytes):
    limit = min(2 * block_bytes + (8 << 20), V7X_VMEM_BYTES - (4 << 20))
    return pltpu.CompilerParams(dimension_semantics=semantics, vmem_limit_bytes=int(limit))


IN_TM = 1024
IN_TN = 1024


IN_CAST_STEPS = 64


def _inproj_kernel(x_ref, nw_ref, w_ref, *refs, n_cast):
    cast_in, z_ref, cast_out, xn_ref = refs[:n_cast], refs[n_cast], refs[n_cast + 1:-1], refs[-1]

    @pl.when(pl.program_id(1) == 0)
    def _():
        x = x_ref[...]
        ms = jnp.mean(x * x, axis=-1, keepdims=True)
        xn_ref[...] = (x * lax.rsqrt(ms + EPS) * nw_ref[...]).astype(BF16)

    z_ref[...] = jnp.dot(xn_ref[...], w_ref[...], preferred_element_type=F32).astype(z_ref.dtype)
    for src_ref, dst_ref in zip(cast_in, cast_out):
        dst_ref[...] = src_ref[...].astype(dst_ref.dtype)


def _inproj(x2d, norm_w, w_in, cast=()):
    m = x2d.shape[0]
    tm, tn = IN_TM, IN_TN
    n_j = IN_COLS // tn
    assert not cast or (m // tm) * n_j >= IN_CAST_STEPS

    def slab(i, j):
        return jnp.minimum(i * n_j + j, IN_CAST_STEPS - 1), 0

    cast_specs = [pl.BlockSpec((w.shape[0] // IN_CAST_STEPS, w.shape[1]), slab) for w in cast]
    cast_bytes = sum(w.size // IN_CAST_STEPS * (4 + 2) for w in cast)
    blocks = tm * D_MODEL * 4 + D_MODEL * tn * 2 + tm * tn * 2 + tm * D_MODEL * 2 + cast_bytes
    z, *casted = pl.pallas_call(
        functools.partial(_inproj_kernel, n_cast=len(cast)),
        out_shape=[jax.ShapeDtypeStruct((m, IN_COLS), BF16)]
        + [jax.ShapeDtypeStruct(w.shape, BF16) for w in cast],
        grid=(m // tm, n_j),
        in_specs=[
            pl.BlockSpec((tm, D_MODEL), lambda i, j: (i, 0)),
            pl.BlockSpec((1, D_MODEL), lambda i, j: (0, 0)),
            pl.BlockSpec((D_MODEL, tn), lambda i, j: (0, j)),
        ] + cast_specs,
        out_specs=[pl.BlockSpec((tm, tn), lambda i, j: (i, j))] + cast_specs,
        scratch_shapes=[pltpu.VMEM((tm, D_MODEL), BF16)],
        compiler_params=_compiler_params(("arbitrary" if cast else "parallel", "arbitrary"), blocks),
        name="norm_inproj",
    )(x2d, norm_w.reshape(1, D_MODEL), w_in, *cast)
    return z, tuple(casted)


NA_SUB = 8
NA_Q_ROWS = 8
LOG2_E = 1.4426950408889634
NA_Q_SCALE = NA_HEAD_DIM ** -0.5 * LOG2_E
NA_K_ROWS = 16
NA_TQ = NA_Q_ROWS * GRID_W
NA_TK = NA_K_ROWS * GRID_W
NA_CQ = 16
NA_CK = 32
NA_N_CHUNKS = GRID_W // NA_CQ
NA_CHUNK_Q = NA_Q_ROWS * NA_CQ
NA_CHUNK_K = NA_K_ROWS * NA_CK
SUBLANES = 8


def _na_key_col_starts():
    starts = []
    for cq in range(NA_N_CHUNKS):
        c = np.arange(cq * NA_CQ, (cq + 1) * NA_CQ)
        c0 = np.clip(c - NA_WIN_COLS // 2, 0, GRID_W - NA_WIN_COLS)
        start = min((int(c0.min()) // SUBLANES) * SUBLANES, GRID_W - NA_CK)
        assert start <= c0.min() and c0.max() + NA_WIN_COLS <= start + NA_CK
        starts.append(start)
    return tuple(starts)


NA_KEY_COL_START = _na_key_col_starts()


NA_INVALID_SLOT = 2 * NA_WIN_ROWS - 1
NA_LANES = 128
NA_LANE_REP = NA_LANES // NA_CK


def _na_row_slots(rows):
    ri = np.arange(NA_Q_ROWS)
    kri = np.arange(NA_K_ROWS)
    half = NA_WIN_ROWS // 2
    n_blocks = rows // NA_Q_ROWS
    row_idx = []
    for kind in range(3):
        q_base = {0: 0, 1: NA_Q_ROWS, 2: (n_blocks - 1) * NA_Q_ROWS}[kind]
        k_base = int(np.clip(q_base - half, 0, rows - NA_K_ROWS))
        r = q_base + ri
        r0 = np.clip(r - half, 0, rows - NA_WIN_ROWS)
        kr = k_base + kri
        valid = (kr[None, :] >= r0[:, None]) & (kr[None, :] < r0[:, None] + NA_WIN_ROWS)
        idx = kr[None, :] - r[:, None] + NA_WIN_ROWS - 1
        row_idx.append(np.where(valid, idx, NA_INVALID_SLOT))
    return np.stack(row_idx)


def _na_column_table(rpb):
    cl = np.arange(NA_CQ)
    kcl = np.arange(NA_CK)
    c = (np.arange(NA_N_CHUNKS) * NA_CQ)[:, None, None] + cl[None, :, None]
    kc = np.asarray(NA_KEY_COL_START)[:, None, None] + kcl[None, None, :]
    c0 = np.clip(c - NA_WIN_COLS // 2, 0, GRID_W - NA_WIN_COLS)
    col_valid = (kc >= c0) & (kc < c0 + NA_WIN_COLS)
    col_idx = np.clip(kc - c + NA_WIN_COLS - 1, 0, 2 * NA_WIN_COLS - 2)
    bias = rpb.astype(F32)[:, :, col_idx] * LOG2_E
    tc = jnp.where(col_valid[None, None], bias, MASK_VALUE)
    tc = jnp.concatenate([tc, jnp.full((NA_HEADS, 1) + tc.shape[2:], MASK_VALUE, F32)], axis=1)
    tc = jnp.transpose(tc, (0, 2, 1, 3, 4))
    return jnp.tile(tc, (1, 1, 1, 1, NA_LANE_REP))


def _na_build_bias(tc_ref, bias_ref, row_slots):
    lane = lax.broadcasted_iota(jnp.int32, (NA_CQ, NA_LANES), 1)
    for kind in range(3):
        for cq in range(NA_N_CHUNKS):
            for ri in range(NA_Q_ROWS):
                for grp in range(NA_K_ROWS // NA_LANE_REP):
                    slots = [int(row_slots[kind, ri, grp * NA_LANE_REP + t]) for t in range(NA_LANE_REP)]
                    tile = tc_ref[cq, slots[-1]]
                    for t in range(NA_LANE_REP - 2, -1, -1):
                        if slots[t] != slots[t + 1]:
                            tile = jnp.where(lane < (t + 1) * NA_CK, tc_ref[cq, slots[t]], tile)
                    bias_ref[kind, cq, ri * NA_CQ:(ri + 1) * NA_CQ, grp * NA_LANES:(grp + 1) * NA_LANES] = tile


def _na_kernel(q_ref, k_ref, v_ref, tc_ref, o_ref, bias_ref, *, seq, row_slots):
    hd = NA_HEAD_DIM
    half_rows = (NA_WIN_ROWS // 2) * GRID_W
    n_blocks = seq // NA_TQ

    @pl.when((pl.program_id(1) == 0) & (pl.program_id(2) == 0))
    def _():
        _na_build_bias(tc_ref, bias_ref, row_slots)

    pairs = [(sb, cq) for sb in range(NA_SUB) for cq in range(NA_N_CHUNKS)]
    qs, ks, vs, kinds = [], [], [], []
    for sb in range(NA_SUB):
        blk = pl.program_id(2) * NA_SUB + sb
        start = pl.multiple_of(jnp.clip(blk * NA_TQ - half_rows, 0, seq - NA_TK), half_rows)
        q = q_ref[sb * NA_TQ:(sb + 1) * NA_TQ, :].astype(F32) * NA_Q_SCALE
        qs.append(q.reshape(NA_Q_ROWS, GRID_W, hd))
        ks.append(k_ref[pl.ds(start, NA_TK), :].astype(F32).reshape(NA_K_ROWS, GRID_W, hd))
        vs.append(v_ref[pl.ds(start, NA_TK), :].astype(F32).reshape(NA_K_ROWS, GRID_W, hd))
        kinds.append(jnp.where(blk == 0, 0, jnp.where(blk == n_blocks - 1, 2, 1)))
    scores = []
    for sb, cq in pairs:
        k0 = NA_KEY_COL_START[cq]
        qc = qs[sb][:, cq * NA_CQ:(cq + 1) * NA_CQ, :].reshape(NA_CHUNK_Q, hd).astype(BF16)
        kc = ks[sb][:, k0:k0 + NA_CK, :].reshape(NA_CHUNK_K, hd).astype(BF16)
        scores.append(lax.dot_general(qc, kc, (((1,), (1,)), ((), ())), preferred_element_type=F32))
    probs, denoms = [], []
    for (sb, cq), s in zip(pairs, scores):
        s = s + bias_ref[kinds[sb], cq]
        p = jnp.exp2(s - jnp.max(s, axis=-1, keepdims=True))
        denoms.append(jnp.sum(p, axis=-1, keepdims=True))
        probs.append(p.astype(BF16))
    outs = []
    for (sb, cq), p, l in zip(pairs, probs, denoms):
        k0 = NA_KEY_COL_START[cq]
        vc = vs[sb][:, k0:k0 + NA_CK, :].reshape(NA_CHUNK_K, hd).astype(BF16)
        o = jnp.dot(p, vc, preferred_element_type=F32) / l
        outs.append(o.reshape(NA_Q_ROWS, NA_CQ, hd))
    for sb in range(NA_SUB):
        o = jnp.concatenate(outs[sb * NA_N_CHUNKS:(sb + 1) * NA_N_CHUNKS], axis=1)
        o_ref[sb * NA_TQ:(sb + 1) * NA_TQ, :] = o.reshape(NA_TQ, hd).astype(o_ref.dtype)


def _neighbourhood_attention(z3, col_table):
    b, seq, _ = z3.shape
    tq = NA_SUB * NA_TQ
    hd = NA_HEAD_DIM
    row_slots = _na_row_slots(seq // GRID_W)
    table_bytes = NA_N_CHUNKS * (NA_INVALID_SLOT + 1) * NA_CQ * NA_LANES * 4
    bias_bytes = 3 * NA_N_CHUNKS * NA_CHUNK_Q * NA_CHUNK_K * 4
    blocks = tq * hd * 2 * 2 + 2 * seq * hd * 2 + table_bytes
    temporaries = NA_SUB * (3 * NA_TK * hd * 4 + 3 * NA_N_CHUNKS * NA_CHUNK_Q * NA_CHUNK_K * 4)
    params = pltpu.CompilerParams(
        dimension_semantics=("arbitrary", "arbitrary", "arbitrary"),
        vmem_limit_bytes=int(2 * blocks + bias_bytes + temporaries + (8 << 20)))
    return pl.pallas_call(
        functools.partial(_na_kernel, seq=seq, row_slots=row_slots),
        out_shape=jax.ShapeDtypeStruct((b, seq, NA_WIDTH), BF16),
        grid=(NA_HEADS, b, seq // tq),
        in_specs=[
            pl.BlockSpec((None, tq, hd), lambda h, bi, s: (bi, s, OFF_NA_Q // hd + h)),
            pl.BlockSpec((None, seq, hd), lambda h, bi, s: (bi, 0, OFF_NA_K // hd + h)),
            pl.BlockSpec((None, seq, hd), lambda h, bi, s: (bi, 0, OFF_NA_V // hd + h)),
            pl.BlockSpec((None, NA_N_CHUNKS, NA_INVALID_SLOT + 1, NA_CQ, NA_LANES),
                         lambda h, bi, s: (h, 0, 0, 0, 0)),
        ],
        out_specs=pl.BlockSpec((None, tq, hd), lambda h, bi, s: (bi, s, h)),
        scratch_shapes=[pltpu.VMEM((3, NA_N_CHUNKS, NA_CHUNK_Q, NA_CHUNK_K), F32)],
        compiler_params=params,
        name="neighbourhood_attention",
    )(z3, z3, z3, col_table)


RET_CHUNK = 256
RET_TB = 2048
RET_NC = RET_TB // RET_CHUNK


def _rope_tables(seq):
    half = RET_QK_DIM // 2
    inv_freq = ROPE_BASE ** (-jnp.arange(half, dtype=F32) / half)
    ang = jnp.arange(seq, dtype=F32)[:, None] * inv_freq[None, :]
    cos, sin = jnp.cos(ang), jnp.sin(ang)
    return jnp.concatenate([cos, cos], axis=-1), jnp.concatenate([-sin, sin], axis=-1)


def _decay_tables(decay, backward):
    c = RET_CHUNK
    lg = jax.nn.log_sigmoid(decay.astype(F32))
    pos = jnp.arange(c, dtype=F32)
    diff = pos[:, None] - pos[None, :]
    lg3 = lg[:, None, None]
    if backward:
        decay_in = jnp.exp(jnp.where(diff < 0, -diff * lg3, -jnp.inf))
        q_dec = jnp.exp((c - pos)[None, :] * lg[:, None])
        k_dec = jnp.exp(pos[None, :] * lg[:, None])
    else:
        decay_in = jnp.exp(jnp.where(diff >= 0, diff * lg3, -jnp.inf))
        q_dec = jnp.exp((pos + 1.0)[None, :] * lg[:, None])
        k_dec = jnp.exp((c - 1.0 - pos)[None, :] * lg[:, None])
    chunk_dec = jnp.exp(c * lg)
    return decay_in, q_dec, k_dec, chunk_dec


def _retention_tables(decay_fwd, decay_bwd):
    c, dk = RET_CHUNK, RET_QK_DIM
    din_f, qd_f, kd_f, cd_f = _decay_tables(decay_fwd, False)
    din_b, qd_b, kd_b, cd_b = _decay_tables(decay_bwd, True)
    qd2 = jnp.concatenate([jnp.broadcast_to(qd_f[:, :, None], (RET_HEADS, c, dk)),
                           jnp.broadcast_to(qd_b[:, :, None], (RET_HEADS, c, dk))], axis=2)
    kdt2 = jnp.concatenate([jnp.broadcast_to(kd_f[:, None, :], (RET_HEADS, dk, c)),
                            jnp.broadcast_to(kd_b[:, None, :], (RET_HEADS, dk, c))], axis=1)
    return din_f + din_b, qd2, kdt2, jnp.stack([cd_f, cd_b])


def _retention_kernel(cd_ref, q_ref, k_ref, v_ref, cos_ref, sin_ref, din_ref, qd_ref, kdt_ref, g_ref,
                      o_ref, statef_ref, stateb_ref, kt_ref, kvf_ref, sb_ref, vc_ref, *, n_blk):
    h = pl.program_id(1)
    pass_id = pl.program_id(2)
    j = pl.program_id(3)
    dk, half = RET_QK_DIM, RET_QK_DIM // 2
    chunks = range(RET_NC)
    slices = [slice(c * RET_CHUNK, (c + 1) * RET_CHUNK) for c in chunks]

    @pl.when(j == 0)
    def _():
        statef_ref[...] = jnp.zeros_like(statef_ref)
        stateb_ref[...] = jnp.zeros_like(stateb_ref)

    def block_rows(blk):
        return pl.ds(pl.multiple_of(blk * RET_TB, RET_TB), RET_TB)

    def rotate(ref, blk):
        x = ref[...].astype(F32)
        return x * cos_ref[block_rows(blk), :] + pltpu.roll(x, half, 1) * sin_ref[block_rows(blk), :]

    @pl.when(pass_id == 0)
    def _():
        blk = n_blk - 1 - j
        kt = (rotate(k_ref, blk) * (dk ** -0.5)).T
        kt_ref[blk] = kt.astype(BF16)
        vc_ref[block_rows(blk), :] = v_ref[...]
        kdt2 = kdt_ref[...]
        kv = [jnp.dot((jnp.concatenate([kt[:, sl], kt[:, sl]], axis=0) * kdt2).astype(BF16), v_ref[sl, :],
                      preferred_element_type=F32) for sl in slices]
        cd_b = cd_ref[1, h]
        state = stateb_ref[...]
        for c in reversed(chunks):
            sb_ref[blk * RET_NC + c] = state.astype(BF16)
            kvf_ref[blk * RET_NC + c] = kv[c][:dk]
            state = cd_b * state + kv[c][dk:]
        stateb_ref[...] = state

    @pl.when(pass_id == 1)
    def _():
        blk = j
        q = rotate(q_ref, blk)
        kt = kt_ref[blk]
        din, qd2 = din_ref[...], qd_ref[...]
        scores = [jnp.dot(q[sl].astype(BF16), kt[:, sl], preferred_element_type=F32) for sl in slices]
        decayed = [(s * din).astype(BF16) for s in scores]
        cd_f = cd_ref[0, h]
        states = [statef_ref[...]]
        for c in chunks:
            states.append(cd_f * states[-1] + kvf_ref[blk * RET_NC + c])
        statef_ref[...] = states[-1]
        for c, sl in zip(chunks, slices):
            lhs = jnp.concatenate(
                [decayed[c], (jnp.concatenate([q[sl], q[sl]], axis=1) * qd2).astype(BF16)], axis=1)
            v_rows = pl.ds(pl.multiple_of(blk * RET_TB + c * RET_CHUNK, RET_CHUNK), RET_CHUNK)
            rhs = jnp.concatenate(
                [vc_ref[v_rows, :], states[c].astype(BF16), sb_ref[blk * RET_NC + c]], axis=0)
            o = jnp.dot(lhs, rhs, preferred_element_type=F32)
            o = o * lax.rsqrt(jnp.mean(o * o, axis=-1, keepdims=True) + EPS)
            g = g_ref[sl, :]
            o_ref[sl, :] = o.astype(BF16) * (g * jax.nn.sigmoid(g))


def _retention(z3, rope, tables):
    b, seq, _ = z3.shape
    n_blk = seq // RET_TB
    dk, dv, c = RET_QK_DIM, RET_V_DIM, RET_CHUNK
    cos2, sin2 = rope
    din, qd, kdt, cd = tables

    def pass1_blk(p, j):
        return jnp.where(p == 0, 0, j)

    def pass0_blk(p, j):
        return jnp.where(p == 0, n_blk - 1 - j, 0)

    n_chunks = seq // c
    blocks = RET_TB * (2 * dk * 2 + dv * 2 + 2 * dv * 2) + 2 * seq * dk * 4 + 3 * c * c * 4
    scratch_bytes = 2 * dk * dv * 4 + seq * dk * 2 + n_chunks * dk * dv * (4 + 2) + seq * dv * 2
    temporaries = 6 * RET_TB * dk * 4 + 16 * c * c * 4
    params = pltpu.CompilerParams(
        dimension_semantics=("parallel", "parallel", "arbitrary", "arbitrary"),
        vmem_limit_bytes=int(2 * blocks + scratch_bytes + temporaries + (8 << 20)))
    return pl.pallas_call(
        functools.partial(_retention_kernel, n_blk=n_blk),
        out_shape=jax.ShapeDtypeStruct((b, seq, RET_V_WIDTH), BF16),
        grid=(b, RET_HEADS, 2, n_blk),
        in_specs=[
            pl.BlockSpec(memory_space=pltpu.SMEM),
            pl.BlockSpec((None, RET_TB, dk), lambda bi, h, p, j: (bi, pass1_blk(p, j), OFF_RET_Q // dk + h)),
            pl.BlockSpec((None, RET_TB, dk), lambda bi, h, p, j: (bi, pass0_blk(p, j), OFF_RET_K // dk + h)),
            pl.BlockSpec((None, RET_TB, dv), lambda bi, h, p, j: (bi, pass0_blk(p, j), OFF_RET_V // dv + h)),
            pl.BlockSpec((seq, dk), lambda bi, h, p, j: (0, 0)),
            pl.BlockSpec((seq, dk), lambda bi, h, p, j: (0, 0)),
            pl.BlockSpec((None, c, c), lambda bi, h, p, j: (h, 0, 0)),
            pl.BlockSpec((None, c, 2 * dk), lambda bi, h, p, j: (h, 0, 0)),
            pl.BlockSpec((None, 2 * dk, c), lambda bi, h, p, j: (h, 0, 0)),
            pl.BlockSpec((None, RET_TB, dv), lambda bi, h, p, j: (bi, pass1_blk(p, j), OFF_RET_G // dv + h)),
        ],
        out_specs=pl.BlockSpec((None, RET_TB, dv), lambda bi, h, p, j: (bi, pass1_blk(p, j), h)),
        scratch_shapes=[
            pltpu.VMEM((dk, dv), F32),
            pltpu.VMEM((dk, dv), F32),
            pltpu.VMEM((n_blk, dk, RET_TB), BF16),
            pltpu.VMEM((n_chunks, dk, dv), F32),
            pltpu.VMEM((n_chunks, dk, dv), BF16),
            pltpu.VMEM((seq, dv), BF16),
        ],
        compiler_params=params,
        name="retention",
    )(cd, z3, z3, z3, cos2, sin2, din, qd, kdt, z3)


MIX_TM = 512
MIX_TN = D_MODEL


MIX_GATE_TN = 1024


def _mix_kernel(a_ref, r_ref, wna_ref, wret_ref, *refs):
    m_ref = refs[-1]
    n_parts = (len(refs) - 1) // 2
    ga_refs, gb_refs = refs[:n_parts], refs[n_parts:2 * n_parts]
    a_in, r_in = a_ref[...], r_ref[...]
    for part, (ga_ref, gb_ref) in enumerate(zip(ga_refs, gb_refs)):
        cols = slice(part * MIX_GATE_TN, (part + 1) * MIX_GATE_TN)
        a = jnp.dot(a_in, wna_ref[:, cols], preferred_element_type=F32)
        r = jnp.dot(r_in, wret_ref[:, cols], preferred_element_type=F32)
        ga = jax.nn.sigmoid(ga_ref[...].astype(F32))
        gb = jax.nn.sigmoid(gb_ref[...].astype(F32))
        m_ref[:, cols] = (ga * a + gb * r).astype(m_ref.dtype)


def _branch_mix(a2d, r2d, z2d, w_na_out, w_ret_out):
    m = a2d.shape[0]
    tm, tn, gtn = MIX_TM, MIX_TN, MIX_GATE_TN
    assert tn % gtn == 0 and OFF_GATE_A % gtn == 0 and OFF_GATE_B % gtn == 0
    parts = tn // gtn

    def gate_specs(offset):
        return [pl.BlockSpec((tm, gtn), functools.partial(lambda i, j, part: (i, offset // gtn + j * parts + part),
                                                          part=part)) for part in range(parts)]

    tiles = tm * NA_WIDTH * 2 + tm * RET_V_WIDTH * 2 + 3 * tm * tn * 2
    weights = (NA_WIDTH + RET_V_WIDTH) * tn * 2
    temporaries = 4 * tm * tn * 4
    resident = pl.Buffered(1) if tn == D_MODEL else None
    params = pltpu.CompilerParams(
        dimension_semantics=("parallel", "arbitrary"),
        vmem_limit_bytes=int(2 * tiles + (1 if resident else 2) * weights + temporaries + (8 << 20)))
    return pl.pallas_call(
        _mix_kernel,
        out_shape=jax.ShapeDtypeStruct((m, D_MODEL), BF16),
        grid=(m // tm, D_MODEL // tn),
        in_specs=[
            pl.BlockSpec((tm, NA_WIDTH), lambda i, j: (i, 0)),
            pl.BlockSpec((tm, RET_V_WIDTH), lambda i, j: (i, 0)),
            pl.BlockSpec((NA_WIDTH, tn), lambda i, j: (0, j), pipeline_mode=resident),
            pl.BlockSpec((RET_V_WIDTH, tn), lambda i, j: (0, j), pipeline_mode=resident),
        ] + gate_specs(OFF_GATE_A) + gate_specs(OFF_GATE_B),
        out_specs=pl.BlockSpec((tm, tn), lambda i, j: (i, j)),
        compiler_params=params,
        name="branch_mix",
    )(a2d, r2d, w_na_out, w_ret_out, *([z2d] * (2 * parts)))


OUT_TM = 512
OUT_TN = D_MODEL


def _outproj_kernel(x_ref, m_ref, w_ref, h_ref):
    h_ref[...] = x_ref[...] + jnp.dot(m_ref[...], w_ref[...], preferred_element_type=F32)


def _outproj(x2d, m2d, w_out):
    m = x2d.shape[0]
    tm, tn = OUT_TM, OUT_TN
    blocks = 2 * tm * tn * 4 + tm * D_MODEL * 2 + D_MODEL * tn * 2 + tm * tn * 4
    return pl.pallas_call(
        _outproj_kernel,
        out_shape=jax.ShapeDtypeStruct((m, D_MODEL), F32),
        grid=(m // tm, D_MODEL // tn),
        in_specs=[
            pl.BlockSpec((tm, tn), lambda i, j: (i, j)),
            pl.BlockSpec((tm, D_MODEL), lambda i, j: (i, 0)),
            pl.BlockSpec((D_MODEL, tn), lambda i, j: (0, j)),
        ],
        out_specs=pl.BlockSpec((tm, tn), lambda i, j: (i, j)),
        compiler_params=_compiler_params(("parallel", "arbitrary"), blocks),
        name="outproj_residual",
    )(x2d, m2d, w_out)


MLP_TM = 512
MLP_TF = 1024


def _mlp_kernel(h_ref, nw_ref, wup_ref, wdown_ref, fw_ref, y_ref, hn_ref, acc_ref):
    f = pl.program_id(1)

    @pl.when(f == 0)
    def _():
        h = h_ref[...]
        ms = jnp.mean(h * h, axis=-1, keepdims=True)
        hn_ref[...] = (h * lax.rsqrt(ms + EPS) * nw_ref[...]).astype(BF16)
        acc_ref[...] = h

    u = jnp.maximum(jnp.dot(hn_ref[...], wup_ref[...], preferred_element_type=F32), 0.0)
    acc_ref[...] += jnp.dot((u * u).astype(BF16), wdown_ref[...], preferred_element_type=F32)

    @pl.when(f == pl.num_programs(1) - 1)
    def _():
        x = acc_ref[...]
        ms = jnp.mean(x * x, axis=-1, keepdims=True)
        y_ref[...] = x * lax.rsqrt(ms + EPS) * fw_ref[...]


def _mlp(h2d, norm_w, w_up, w_down, final_w):
    m = h2d.shape[0]
    tm, tf = MLP_TM, MLP_TF
    blocks = 2 * tm * D_MODEL * 4 + 2 * D_MODEL * tf * 2 + tm * D_MODEL * 2 + tm * D_MODEL * 4 + 2 * tm * tf * 4
    return pl.pallas_call(
        _mlp_kernel,
        out_shape=jax.ShapeDtypeStruct((m, D_MODEL), F32),
        grid=(m // tm, D_FF // tf),
        in_specs=[
            pl.BlockSpec((tm, D_MODEL), lambda i, f: (i, 0)),
            pl.BlockSpec((1, D_MODEL), lambda i, f: (0, 0)),
            pl.BlockSpec((D_MODEL, tf), lambda i, f: (0, f)),
            pl.BlockSpec((tf, D_MODEL), lambda i, f: (f, 0)),
            pl.BlockSpec((1, D_MODEL), lambda i, f: (0, 0)),
        ],
        out_specs=pl.BlockSpec((tm, D_MODEL), lambda i, f: (i, 0)),
        scratch_shapes=[pltpu.VMEM((tm, D_MODEL), BF16), pltpu.VMEM((tm, D_MODEL), F32)],
        compiler_params=_compiler_params(("parallel", "arbitrary"), blocks),
        name="mlp_final_norm",
    )(h2d, norm_w.reshape(1, D_MODEL), w_up, w_down, final_w.reshape(1, D_MODEL))


LATE_WEIGHTS = ("w_na_out", "w_ret_out", "w_out", "w_mlp_up", "w_mlp_down")


def _encode(x, p):
    b, seq, d = x.shape
    x2d = x.reshape(b * seq, d)
    late = [k for k in LATE_WEIGHTS if p[k].dtype != BF16]
    z2d, casted = _inproj(x2d, p["norm_mix_w"], p["w_in"], cast=tuple(p[k] for k in late))
    p = {**p, **dict(zip(late, casted))}
    z3 = z2d.reshape(b, seq, IN_COLS)
    a = _neighbourhood_attention(z3, p["na_bias"])
    r = _retention(z3, p["rope"], p["ret_tables"])
    m2d = _branch_mix(a.reshape(b * seq, NA_WIDTH), r.reshape(b * seq, RET_V_WIDTH), z2d,
                      p["w_na_out"], p["w_ret_out"])
    h2d = _outproj(x2d, m2d, p["w_out"])
    y2d = _mlp(h2d, p["norm_mlp_w"], p["w_mlp_up"], p["w_mlp_down"], p["norm_final_w"])
    return y2d.reshape(b, seq, d), p


def kernel(x_prompt, x_sample, norm_mix_w, w_in, na_rpb, ret_decay_fwd, ret_decay_bwd, w_na_out, w_ret_out,
           w_out, norm_mlp_w, w_mlp_up, w_mlp_down, norm_final_w):
    depth = w_in.shape[0]
    seq = x_prompt.shape[1]
    assert x_sample.shape[1] == seq and seq % RET_TB == 0 and seq % (NA_SUB * NA_TQ) == 0
    assert depth == 1, "the final norm is fused into the last layer's MLP kernel"
    rope = _rope_tables(seq)
    l = 0
    params = {
        "norm_mix_w": norm_mix_w[l].astype(F32),
        "w_in": w_in[l].astype(BF16),
        "na_bias": _na_column_table(na_rpb[l]),
        "rope": rope,
        "ret_tables": _retention_tables(ret_decay_fwd[l], ret_decay_bwd[l]),
        "w_na_out": w_na_out[l].astype(F32),
        "w_ret_out": w_ret_out[l].astype(F32),
        "w_out": w_out[l].astype(F32),
        "norm_mlp_w": norm_mlp_w[l].astype(F32),
        "w_mlp_up": w_mlp_up[l].astype(F32),
        "w_mlp_down": w_mlp_down[l].astype(F32),
        "norm_final_w": norm_final_w.astype(F32),
    }
    y_prompt, params = _encode(x_prompt, params)
    y_sample, _ = _encode(x_sample, params)
    return y_prompt, y_sample
```

```python
import functools

import jax
import jax.numpy as jnp
import numpy as np
from jax import lax
from jax.experimental import pallas as pl
from jax.experimental.pallas import tpu as pltpu

F32 = jnp.float32
BF16 = jnp.bfloat16

D_MODEL = 2048
GRID_W = 64
NA_HEADS = 8
NA_HEAD_DIM = 128
NA_WIDTH = NA_HEADS * NA_HEAD_DIM
NA_WIN_ROWS = 8
NA_WIN_COLS = 16
RET_HEADS = 8
RET_QK_DIM = 128
RET_V_DIM = 256
RET_QK_WIDTH = RET_HEADS * RET_QK_DIM
RET_V_WIDTH = RET_HEADS * RET_V_DIM
ROPE_BASE = 10000.0
D_FF = 4 * D_MODEL
EPS = 1e-6

OFF_NA_Q = 0
OFF_NA_K = OFF_NA_Q + NA_WIDTH
OFF_NA_V = OFF_NA_K + NA_WIDTH
OFF_RET_Q = OFF_NA_V + NA_WIDTH
OFF_RET_K = OFF_RET_Q + RET_QK_WIDTH
OFF_RET_V = OFF_RET_K + RET_QK_WIDTH
OFF_RET_G = OFF_RET_V + RET_V_WIDTH
OFF_GATE_A = OFF_RET_G + RET_V_WIDTH
OFF_GATE_B = OFF_GATE_A + D_MODEL
IN_COLS = OFF_GATE_B + D_MODEL

MASK_VALUE = -1e30
V7X_VMEM_BYTES = 64 * 1024 * 1024


def _compiler_params(semantics, block_bytes):
    limit = min(2 * block_bytes + (8 << 20), V7X_VMEM_BYTES - (4 << 20))
    return pltpu.CompilerParams(dimension_semantics=semantics, vmem_limit_bytes=int(limit))


IN_TM = 1024
IN_TN = 1024


IN_CAST_STEPS = 64


def _inproj_kernel(x_ref, nw_ref, w_ref, *refs, n_cast):
    cast_in, z_ref, cast_out, xn_ref = refs[:n_cast], refs[n_cast], refs[n_cast + 1:-1], refs[-1]

    @pl.when(pl.program_id(1) == 0)
    def _():
        x = x_ref[...]
        ms = jnp.mean(x * x, axis=-1, keepdims=True)
        xn_ref[...] = (x * lax.rsqrt(ms + EPS) * nw_ref[...]).astype(BF16)

    z_ref[...] = jnp.dot(xn_ref[...], w_ref[...], preferred_element_type=F32).astype(z_ref.dtype)
    for src_ref, dst_ref in zip(cast_in, cast_out):
        dst_ref[...] = src_ref[...].astype(dst_ref.dtype)


def _inproj(x2d, norm_w, w_in, cast=()):
    m = x2d.shape[0]
    tm, tn = IN_TM, IN_TN
    n_j = IN_COLS // tn
    assert not cast or (m // tm) * n_j >= IN_CAST_STEPS

    def slab(i, j):
        return jnp.minimum(i * n_j + j, IN_CAST_STEPS - 1), 0

    cast_specs = [pl.BlockSpec((w.shape[0] // IN_CAST_STEPS, w.shape[1]), slab) for w in cast]
    cast_bytes = sum(w.size // IN_CAST_STEPS * (4 + 2) for w in cast)
    blocks = tm * D_MODEL * 4 + D_MODEL * tn * 2 + tm * tn * 2 + tm * D_MODEL * 2 + cast_bytes
    z, *casted = pl.pallas_call(
        functools.partial(_inproj_kernel, n_cast=len(cast)),
        out_shape=[jax.ShapeDtypeStruct((m, IN_COLS), BF16)]
        + [jax.ShapeDtypeStruct(w.shape, BF16) for w in cast],
        grid=(m // tm, n_j),
        in_specs=[
            pl.BlockSpec((tm, D_MODEL), lambda i, j: (i, 0)),
            pl.BlockSpec((1, D_MODEL), lambda i, j: (0, 0)),
            pl.BlockSpec((D_MODEL, tn), lambda i, j: (0, j)),
        ] + cast_specs,
        out_specs=[pl.BlockSpec((tm, tn), lambda i, j: (i, j))] + cast_specs,
        scratch_shapes=[pltpu.VMEM((tm, D_MODEL), BF16)],
        compiler_params=_compiler_params(("arbitrary" if cast else "parallel", "arbitrary"), blocks),
        name="norm_inproj",
    )(x2d, norm_w.reshape(1, D_MODEL), w_in, *cast)
    return z, tuple(casted)


NA_SUB = 8
NA_Q_ROWS = 8
LOG2_E = 1.4426950408889634
NA_Q_SCALE = NA_HEAD_DIM ** -0.5 * LOG2_E
NA_K_ROWS = 16
NA_TQ = NA_Q_ROWS * GRID_W
NA_TK = NA_K_ROWS * GRID_W
NA_CQ = 16
NA_CK = 32
NA_N_CHUNKS = GRID_W // NA_CQ
NA_CHUNK_Q = NA_Q_ROWS * NA_CQ
NA_CHUNK_K = NA_K_ROWS * NA_CK
SUBLANES = 8


def _na_key_col_starts():
    starts = []
    for cq in range(NA_N_CHUNKS):
        c = np.arange(cq * NA_CQ, (cq + 1) * NA_CQ)
        c0 = np.clip(c - NA_WIN_COLS // 2, 0, GRID_W - NA_WIN_COLS)
        start = min((int(c0.min()) // SUBLANES) * SUBLANES, GRID_W - NA_CK)
        assert start <= c0.min() and c0.max() + NA_WIN_COLS <= start + NA_CK
        starts.append(start)
    return tuple(starts)


NA_KEY_COL_START = _na_key_col_starts()


NA_INVALID_SLOT = 2 * NA_WIN_ROWS - 1
NA_LANES = 128
NA_LANE_REP = NA_LANES // NA_CK


def _na_row_slots(rows):
    ri = np.arange(NA_Q_ROWS)
    kri = np.arange(NA_K_ROWS)
    half = NA_WIN_ROWS // 2
    n_blocks = rows // NA_Q_ROWS
    row_idx = []
    for kind in range(3):
        q_base = {0: 0, 1: NA_Q_ROWS, 2: (n_blocks - 1) * NA_Q_ROWS}[kind]
        k_base = int(np.clip(q_base - half, 0, rows - NA_K_ROWS))
        r = q_base + ri
        r0 = np.clip(r - half, 0, rows - NA_WIN_ROWS)
        kr = k_base + kri
        valid = (kr[None, :] >= r0[:, None]) & (kr[None, :] < r0[:, None] + NA_WIN_ROWS)
        idx = kr[None, :] - r[:, None] + NA_WIN_ROWS - 1
        row_idx.append(np.where(valid, idx, NA_INVALID_SLOT))
    return np.stack(row_idx)


def _na_column_table(rpb):
    cl = np.arange(NA_CQ)
    kcl = np.arange(NA_CK)
    c = (np.arange(NA_N_CHUNKS) * NA_CQ)[:, None, None] + cl[None, :, None]
    kc = np.asarray(NA_KEY_COL_START)[:, None, None] + kcl[None, None, :]
    c0 = np.clip(c - NA_WIN_COLS // 2, 0, GRID_W - NA_WIN_COLS)
    col_valid = (kc >= c0) & (kc < c0 + NA_WIN_COLS)
    col_idx = np.clip(kc - c + NA_WIN_COLS - 1, 0, 2 * NA_WIN_COLS - 2)
    bias = rpb.astype(F32)[:, :, col_idx] * LOG2_E
    tc = jnp.where(col_valid[None, None], bias, MASK_VALUE)
    tc = jnp.concatenate([tc, jnp.full((NA_HEADS, 1) + tc.shape[2:], MASK_VALUE, F32)], axis=1)
    tc = jnp.transpose(tc, (0, 2, 1, 3, 4))
    return jnp.tile(tc, (1, 1, 1, 1, NA_LANE_REP))


def _na_build_bias(tc_ref, bias_ref, row_slots):
    lane = lax.broadcasted_iota(jnp.int32, (NA_CQ, NA_LANES), 1)
    for kind in range(3):
        for cq in range(NA_N_CHUNKS):
            for ri in range(NA_Q_ROWS):
                for grp in range(NA_K_ROWS // NA_LANE_REP):
                    slots = [int(row_slots[kind, ri, grp * NA_LANE_REP + t]) for t in range(NA_LANE_REP)]
                    tile = tc_ref[cq, slots[-1]]
                    for t in range(NA_LANE_REP - 2, -1, -1):
                        if slots[t] != slots[t + 1]:
                            tile = jnp.where(lane < (t + 1) * NA_CK, tc_ref[cq, slots[t]], tile)
                    bias_ref[kind, cq, ri * NA_CQ:(ri + 1) * NA_CQ, grp * NA_LANES:(grp + 1) * NA_LANES] = tile


def _na_kernel(q_ref, k_ref, v_ref, tc_ref, o_ref, bias_ref, *, seq, row_slots):
    hd = NA_HEAD_DIM
    half_rows = (NA_WIN_ROWS // 2) * GRID_W
    n_blocks = seq // NA_TQ

    @pl.when((pl.program_id(1) == 0) & (pl.program_id(2) == 0))
    def _():
        _na_build_bias(tc_ref, bias_ref, row_slots)

    pairs = [(sb, cq) for sb in range(NA_SUB) for cq in range(NA_N_CHUNKS)]
    qs, ks, vs, kinds = [], [], [], []
    for sb in range(NA_SUB):
        blk = pl.program_id(2) * NA_SUB + sb
        start = pl.multiple_of(jnp.clip(blk * NA_TQ - half_rows, 0, seq - NA_TK), half_rows)
        q = q_ref[sb * NA_TQ:(sb + 1) * NA_TQ, :].astype(F32) * NA_Q_SCALE
        qs.append(q.reshape(NA_Q_ROWS, GRID_W, hd))
        ks.append(k_ref[pl.ds(start, NA_TK), :].astype(F32).reshape(NA_K_ROWS, GRID_W, hd))
        vs.append(v_ref[pl.ds(start, NA_TK), :].astype(F32).reshape(NA_K_ROWS, GRID_W, hd))
        kinds.append(jnp.where(blk == 0, 0, jnp.where(blk == n_blocks - 1, 2, 1)))
    scores = []
    for sb, cq in pairs:
        k0 = NA_KEY_COL_START[cq]
        qc = qs[sb][:, cq * NA_CQ:(cq + 1) * NA_CQ, :].reshape(NA_CHUNK_Q, hd).astype(BF16)
        kc = ks[sb][:, k0:k0 + NA_CK, :].reshape(NA_CHUNK_K, hd).astype(BF16)
        scores.append(lax.dot_general(qc, kc, (((1,), (1,)), ((), ())), preferred_element_type=F32))
    probs, denoms = [], []
    for (sb, cq), s in zip(pairs, scores):
        s = s + bias_ref[kinds[sb], cq]
        p = jnp.exp2(s - jnp.max(s, axis=-1, keepdims=True))
        denoms.append(jnp.sum(p, axis=-1, keepdims=True))
        probs.append(p.astype(BF16))
    outs = []
    for (sb, cq), p, l in zip(pairs, probs, denoms):
        k0 = NA_KEY_COL_START[cq]
        vc = vs[sb][:, k0:k0 + NA_CK, :].reshape(NA_CHUNK_K, hd).astype(BF16)
        o = jnp.dot(p, vc, preferred_element_type=F32) / l
        outs.append(o.reshape(NA_Q_ROWS, NA_CQ, hd))
    for sb in range(NA_SUB):
        o = jnp.concatenate(outs[sb * NA_N_CHUNKS:(sb + 1) * NA_N_CHUNKS], axis=1)
        o_ref[sb * NA_TQ:(sb + 1) * NA_TQ, :] = o.reshape(NA_TQ, hd).astype(o_ref.dtype)


def _neighbourhood_attention(z3, col_table):
    b, seq, _ = z3.shape
    tq = NA_SUB * NA_TQ
    hd = NA_HEAD_DIM
    row_slots = _na_row_slots(seq // GRID_W)
    table_bytes = NA_N_CHUNKS * (NA_INVALID_SLOT + 1) * NA_CQ * NA_LANES * 4
    bias_bytes = 3 * NA_N_CHUNKS * NA_CHUNK_Q * NA_CHUNK_K * 4
    blocks = tq * hd * 2 * 2 + 2 * seq * hd * 2 + table_bytes
    temporaries = NA_SUB * (3 * NA_TK * hd * 4 + 3 * NA_N_CHUNKS * NA_CHUNK_Q * NA_CHUNK_K * 4)
    params = pltpu.CompilerParams(
        dimension_semantics=("arbitrary", "arbitrary", "arbitrary"),
        vmem_limit_bytes=int(2 * blocks + bias_bytes + temporaries + (8 << 20)))
    return pl.pallas_call(
        functools.partial(_na_kernel, seq=seq, row_slots=row_slots),
        out_shape=jax.ShapeDtypeStruct((b, seq, NA_WIDTH), BF16),
        grid=(NA_HEADS, b, seq // tq),
        in_specs=[
            pl.BlockSpec((None, tq, hd), lambda h, bi, s: (bi, s, OFF_NA_Q // hd + h)),
            pl.BlockSpec((None, seq, hd), lambda h, bi, s: (bi, 0, OFF_NA_K // hd + h)),
            pl.BlockSpec((None, seq, hd), lambda h, bi, s: (bi, 0, OFF_NA_V // hd + h)),
            pl.BlockSpec((None, NA_N_CHUNKS, NA_INVALID_SLOT + 1, NA_CQ, NA_LANES),
                         lambda h, bi, s: (h, 0, 0, 0, 0)),
        ],
        out_specs=pl.BlockSpec((None, tq, hd), lambda h, bi, s: (bi, s, h)),
        scratch_shapes=[pltpu.VMEM((3, NA_N_CHUNKS, NA_CHUNK_Q, NA_CHUNK_K), F32)],
        compiler_params=params,
        name="neighbourhood_attention",
    )(z3, z3, z3, col_table)


RET_CHUNK = 256
RET_TB = 4096
RET_NC = RET_TB // RET_CHUNK


def _rope_tables(seq):
    half = RET_QK_DIM // 2
    inv_freq = ROPE_BASE ** (-jnp.arange(half, dtype=F32) / half)
    ang = jnp.arange(seq, dtype=F32)[:, None] * inv_freq[None, :]
    cos, sin = jnp.cos(ang), jnp.sin(ang)
    return jnp.concatenate([cos, cos], axis=-1), jnp.concatenate([-sin, sin], axis=-1)


def _decay_tables(decay, backward):
    c = RET_CHUNK
    lg = jax.nn.log_sigmoid(decay.astype(F32))
    pos = jnp.arange(c, dtype=F32)
    diff = pos[:, None] - pos[None, :]
    lg3 = lg[:, None, None]
    if backward:
        decay_in = jnp.exp(jnp.where(diff < 0, -diff * lg3, -jnp.inf))
        q_dec = jnp.exp((c - pos)[None, :] * lg[:, None])
        k_dec = jnp.exp(pos[None, :] * lg[:, None])
    else:
        decay_in = jnp.exp(jnp.where(diff >= 0, diff * lg3, -jnp.inf))
        q_dec = jnp.exp((pos + 1.0)[None, :] * lg[:, None])
        k_dec = jnp.exp((c - 1.0 - pos)[None, :] * lg[:, None])
    chunk_dec = jnp.exp(c * lg)
    return decay_in, q_dec, k_dec, chunk_dec


def _retention_tables(decay_fwd, decay_bwd):
    c, dk = RET_CHUNK, RET_QK_DIM
    din_f, qd_f, kd_f, cd_f = _decay_tables(decay_fwd, False)
    din_b, qd_b, kd_b, cd_b = _decay_tables(decay_bwd, True)
    qd2 = jnp.concatenate([jnp.broadcast_to(qd_f[:, :, None], (RET_HEADS, c, dk)),
                           jnp.broadcast_to(qd_b[:, :, None], (RET_HEADS, c, dk))], axis=2)
    kdt2 = jnp.concatenate([jnp.broadcast_to(kd_f[:, None, :], (RET_HEADS, dk, c)),
                            jnp.broadcast_to(kd_b[:, None, :], (RET_HEADS, dk, c))], axis=1)
    return din_f + din_b, qd2, kdt2, jnp.stack([cd_f, cd_b])


def _retention_kernel(cd_ref, q_ref, k_ref, v_ref, cos_ref, sin_ref, din_ref, qd_ref, kdt_ref, g_ref,
                      o_ref, statef_ref, stateb_ref, kt_ref, kvf_ref, sb_ref, vc_ref, *, n_blk):
    h = pl.program_id(1)
    pass_id = pl.program_id(2)
    j = pl.program_id(3)
    dk, half = RET_QK_DIM, RET_QK_DIM // 2
    chunks = range(RET_NC)
    slices = [slice(c * RET_CHUNK, (c + 1) * RET_CHUNK) for c in chunks]

    @pl.when(j == 0)
    def _():
        statef_ref[...] = jnp.zeros_like(statef_ref)
        stateb_ref[...] = jnp.zeros_like(stateb_ref)

    def block_rows(blk):
        return pl.ds(pl.multiple_of(blk * RET_TB, RET_TB), RET_TB)

    def rotate(ref, blk):
        x = ref[...].astype(F32)
        return x * cos_ref[block_rows(blk), :] + pltpu.roll(x, half, 1) * sin_ref[block_rows(blk), :]

    @pl.when(pass_id == 0)
    def _():
        blk = n_blk - 1 - j
        kt = (rotate(k_ref, blk) * (dk ** -0.5)).T
        kt_ref[blk] = kt.astype(BF16)
        vc_ref[block_rows(blk), :] = v_ref[...]
        kdt2 = kdt_ref[...]
        kv = [jnp.dot((jnp.concatenate([kt[:, sl], kt[:, sl]], axis=0) * kdt2).astype(BF16), v_ref[sl, :],
                      preferred_element_type=F32) for sl in slices]
        cd_b = cd_ref[1, h]
        state = stateb_ref[...]
        for c in reversed(chunks):
            sb_ref[blk * RET_NC + c] = state.astype(BF16)
            kvf_ref[blk * RET_NC + c] = kv[c][:dk]
            state = cd_b * state + kv[c][dk:]
        stateb_ref[...] = state

    @pl.when(pass_id == 1)
    def _():
        blk = j
        q = rotate(q_ref, blk)
        kt = kt_ref[blk]
        din, qd2 = din_ref[...], qd_ref[...]
        scores = [jnp.dot(q[sl].astype(BF16), kt[:, sl], preferred_element_type=F32) for sl in slices]
        decayed = [(s * din).astype(BF16) for s in scores]
        cd_f = cd_ref[0, h]
        states = [statef_ref[...]]
        for c in chunks:
            states.append(cd_f * states[-1] + kvf_ref[blk * RET_NC + c])
        statef_ref[...] = states[-1]
        for c, sl in zip(chunks, slices):
            lhs = jnp.concatenate(
                [decayed[c], (jnp.concatenate([q[sl], q[sl]], axis=1) * qd2).astype(BF16)], axis=1)
            v_rows = pl.ds(pl.multiple_of(blk * RET_TB + c * RET_CHUNK, RET_CHUNK), RET_CHUNK)
            rhs = jnp.concatenate(
                [vc_ref[v_rows, :], states[c].astype(BF16), sb_ref[blk * RET_NC + c]], axis=0)
            o = jnp.dot(lhs, rhs, preferred_element_type=F32)
            o = o * lax.rsqrt(jnp.mean(o * o, axis=-1, keepdims=True) + EPS)
            g = g_ref[sl, :]
            o_ref[sl, :] = o.astype(BF16) * (g * jax.nn.sigmoid(g))


def _retention(z3, rope, tables):
    b, seq, _ = z3.shape
    n_blk = seq // RET_TB
    dk, dv, c = RET_QK_DIM, RET_V_DIM, RET_CHUNK
    cos2, sin2 = rope
    din, qd, kdt, cd = tables

    def pass1_blk(p, j):
        return jnp.where(p == 0, 0, j)

    def pass0_blk(p, j):
        return jnp.where(p == 0, n_blk - 1 - j, 0)

    n_chunks = seq // c
    blocks = RET_TB * (2 * dk * 2 + dv * 2 + 2 * dv * 2) + seq * dk * 4 + 3 * c * c * 4
    scratch_bytes = 2 * dk * dv * 4 + seq * dk * 2 + n_chunks * dk * dv * (4 + 2) + seq * dv * 2
    temporaries = 6 * RET_TB * dk * 4 + 16 * c * c * 4
    params = pltpu.CompilerParams(
        dimension_semantics=("parallel", "parallel", "arbitrary", "arbitrary"),
        vmem_limit_bytes=int(min(2 * blocks + scratch_bytes + temporaries + (8 << 20),
                                 V7X_VMEM_BYTES - (4 << 20))))
    return pl.pallas_call(
        functools.partial(_retention_kernel, n_blk=n_blk),
        out_shape=jax.ShapeDtypeStruct((b, seq, RET_V_WIDTH), BF16),
        grid=(b, RET_HEADS, 2, n_blk),
        in_specs=[
            pl.BlockSpec(memory_space=pltpu.SMEM),
            pl.BlockSpec((None, RET_TB, dk), lambda bi, h, p, j: (bi, pass1_blk(p, j), OFF_RET_Q // dk + h)),
            pl.BlockSpec((None, RET_TB, dk), lambda bi, h, p, j: (bi, pass0_blk(p, j), OFF_RET_K // dk + h)),
            pl.BlockSpec((None, RET_TB, dv), lambda bi, h, p, j: (bi, pass0_blk(p, j), OFF_RET_V // dv + h)),
            pl.BlockSpec((seq, dk), lambda bi, h, p, j: (0, 0), pipeline_mode=pl.Buffered(1)),
            pl.BlockSpec((seq, dk), lambda bi, h, p, j: (0, 0), pipeline_mode=pl.Buffered(1)),
            pl.BlockSpec((None, c, c), lambda bi, h, p, j: (h, 0, 0)),
            pl.BlockSpec((None, c, 2 * dk), lambda bi, h, p, j: (h, 0, 0)),
            pl.BlockSpec((None, 2 * dk, c), lambda bi, h, p, j: (h, 0, 0)),
            pl.BlockSpec((None, RET_TB, dv), lambda bi, h, p, j: (bi, pass1_blk(p, j), OFF_RET_G // dv + h)),
        ],
        out_specs=pl.BlockSpec((None, RET_TB, dv), lambda bi, h, p, j: (bi, pass1_blk(p, j), h)),
        scratch_shapes=[
            pltpu.VMEM((dk, dv), F32),
            pltpu.VMEM((dk, dv), F32),
            pltpu.VMEM((n_blk, dk, RET_TB), BF16),
            pltpu.VMEM((n_chunks, dk, dv), F32),
            pltpu.VMEM((n_chunks, dk, dv), BF16),
            pltpu.VMEM((seq, dv), BF16),
        ],
        compiler_params=params,
        name="retention",
    )(cd, z3, z3, z3, cos2, sin2, din, qd, kdt, z3)


MIX_TM = 512
MIX_TN = D_MODEL


MIX_GATE_TN = 1024


def _mix_kernel(a_ref, r_ref, wna_ref, wret_ref, *refs):
    m_ref = refs[-1]
    n_parts = (len(refs) - 1) // 2
    ga_refs, gb_refs = refs[:n_parts], refs[n_parts:2 * n_parts]
    a_in, r_in = a_ref[...], r_ref[...]
    for part, (ga_ref, gb_ref) in enumerate(zip(ga_refs, gb_refs)):
        cols = slice(part * MIX_GATE_TN, (part + 1) * MIX_GATE_TN)
        a = jnp.dot(a_in, wna_ref[:, cols], preferred_element_type=F32)
        r = jnp.dot(r_in, wret_ref[:, cols], preferred_element_type=F32)
        ga = jax.nn.sigmoid(ga_ref[...].astype(F32))
        gb = jax.nn.sigmoid(gb_ref[...].astype(F32))
        m_ref[:, cols] = (ga * a + gb * r).astype(m_ref.dtype)


def _branch_mix(a2d, r2d, z2d, w_na_out, w_ret_out):
    m = a2d.shape[0]
    tm, tn, gtn = MIX_TM, MIX_TN, MIX_GATE_TN
    assert tn % gtn == 0 and OFF_GATE_A % gtn == 0 and OFF_GATE_B % gtn == 0
    parts = tn // gtn

    def gate_specs(offset):
        return [pl.BlockSpec((tm, gtn), functools.partial(lambda i, j, part: (i, offset // gtn + j * parts + part),
                                                          part=part)) for part in range(parts)]

    tiles = tm * NA_WIDTH * 2 + tm * RET_V_WIDTH * 2 + 3 * tm * tn * 2
    weights = (NA_WIDTH + RET_V_WIDTH) * tn * 2
    temporaries = 4 * tm * tn * 4
    resident = pl.Buffered(1) if tn == D_MODEL else None
    params = pltpu.CompilerParams(
        dimension_semantics=("parallel", "arbitrary"),
        vmem_limit_bytes=int(2 * tiles + (1 if resident else 2) * weights + temporaries + (8 << 20)))
    return pl.pallas_call(
        _mix_kernel,
        out_shape=jax.ShapeDtypeStruct((m, D_MODEL), BF16),
        grid=(m // tm, D_MODEL // tn),
        in_specs=[
            pl.BlockSpec((tm, NA_WIDTH), lambda i, j: (i, 0)),
            pl.BlockSpec((tm, RET_V_WIDTH), lambda i, j: (i, 0)),
            pl.BlockSpec((NA_WIDTH, tn), lambda i, j: (0, j), pipeline_mode=resident),
            pl.BlockSpec((RET_V_WIDTH, tn), lambda i, j: (0, j), pipeline_mode=resident),
        ] + gate_specs(OFF_GATE_A) + gate_specs(OFF_GATE_B),
        out_specs=pl.BlockSpec((tm, tn), lambda i, j: (i, j)),
        compiler_params=params,
        name="branch_mix",
    )(a2d, r2d, w_na_out, w_ret_out, *([z2d] * (2 * parts)))


OUT_TM = 512
OUT_TN = D_MODEL


def _outproj_kernel(x_ref, m_ref, w_ref, h_ref):
    h_ref[...] = x_ref[...] + jnp.dot(m_ref[...], w_ref[...], preferred_element_type=F32)


def _outproj(x2d, m2d, w_out):
    m = x2d.shape[0]
    tm, tn = OUT_TM, OUT_TN
    blocks = 2 * tm * tn * 4 + tm * D_MODEL * 2 + D_MODEL * tn * 2 + tm * tn * 4
    return pl.pallas_call(
        _outproj_kernel,
        out_shape=jax.ShapeDtypeStruct((m, D_MODEL), F32),
        grid=(m // tm, D_MODEL // tn),
        in_specs=[
            pl.BlockSpec((tm, tn), lambda i, j: (i, j)),
            pl.BlockSpec((tm, D_MODEL), lambda i, j: (i, 0)),
            pl.BlockSpec((D_MODEL, tn), lambda i, j: (0, j)),
        ],
        out_specs=pl.BlockSpec((tm, tn), lambda i, j: (i, j)),
        compiler_params=_compiler_params(("parallel", "arbitrary"), blocks),
        name="outproj_residual",
    )(x2d, m2d, w_out)


MLP_TM = 512
MLP_TF = 1024


def _mlp_kernel(h_ref, nw_ref, wup_ref, wdown_ref, fw_ref, y_ref, hn_ref, acc_ref):
    f = pl.program_id(1)

    @pl.when(f == 0)
    def _():
        h = h_ref[...]
        ms = jnp.mean(h * h, axis=-1, keepdims=True)
        hn_ref[...] = (h * lax.rsqrt(ms + EPS) * nw_ref[...]).astype(BF16)
        acc_ref[...] = h

    u = jnp.maximum(jnp.dot(hn_ref[...], wup_ref[...], preferred_element_type=F32), 0.0)
    acc_ref[...] += jnp.dot((u * u).astype(BF16), wdown_ref[...], preferred_element_type=F32)

    @pl.when(f == pl.num_programs(1) - 1)
    def _():
        x = acc_ref[...]
        ms = jnp.mean(x * x, axis=-1, keepdims=True)
        y_ref[...] = x * lax.rsqrt(ms + EPS) * fw_ref[...]


def _mlp(h2d, norm_w, w_up, w_down, final_w):
    m = h2d.shape[0]
    tm, tf = MLP_TM, MLP_TF
    blocks = 2 * tm * D_MODEL * 4 + 2 * D_MODEL * tf * 2 + tm * D_MODEL * 2 + tm * D_MODEL * 4 + 2 * tm * tf * 4
    return pl.pallas_call(
        _mlp_kernel,
        out_shape=jax.ShapeDtypeStruct((m, D_MODEL), F32),
        grid=(m // tm, D_FF // tf),
        in_specs=[
            pl.BlockSpec((tm, D_MODEL), lambda i, f: (i, 0)),
            pl.BlockSpec((1, D_MODEL), lambda i, f: (0, 0)),
            pl.BlockSpec((D_MODEL, tf), lambda i, f: (0, f)),
            pl.BlockSpec((tf, D_MODEL), lambda i, f: (f, 0)),
            pl.BlockSpec((1, D_MODEL), lambda i, f: (0, 0)),
        ],
        out_specs=pl.BlockSpec((tm, D_MODEL), lambda i, f: (i, 0)),
        scratch_shapes=[pltpu.VMEM((tm, D_MODEL), BF16), pltpu.VMEM((tm, D_MODEL), F32)],
        compiler_params=_compiler_params(("parallel", "arbitrary"), blocks),
        name="mlp_final_norm",
    )(h2d, norm_w.reshape(1, D_MODEL), w_up, w_down, final_w.reshape(1, D_MODEL))


LATE_WEIGHTS = ("w_na_out", "w_ret_out", "w_out", "w_mlp_up", "w_mlp_down")


def _encode(x, p):
    b, seq, d = x.shape
    x2d = x.reshape(b * seq, d)
    late = [k for k in LATE_WEIGHTS if p[k].dtype != BF16]
    z2d, casted = _inproj(x2d, p["norm_mix_w"], p["w_in"], cast=tuple(p[k] for k in late))
    p = {**p, **dict(zip(late, casted))}
    z3 = z2d.reshape(b, seq, IN_COLS)
    a = _neighbourhood_attention(z3, p["na_bias"])
    r = _retention(z3, p["rope"], p["ret_tables"])
    m2d = _branch_mix(a.reshape(b * seq, NA_WIDTH), r.reshape(b * seq, RET_V_WIDTH), z2d,
                      p["w_na_out"], p["w_ret_out"])
    h2d = _outproj(x2d, m2d, p["w_out"])
    y2d = _mlp(h2d, p["norm_mlp_w"], p["w_mlp_up"], p["w_mlp_down"], p["norm_final_w"])
    return y2d.reshape(b, seq, d), p


def kernel(x_prompt, x_sample, norm_mix_w, w_in, na_rpb, ret_decay_fwd, ret_decay_bwd, w_na_out, w_ret_out,
           w_out, norm_mlp_w, w_mlp_up, w_mlp_down, norm_final_w):
    depth = w_in.shape[0]
    seq = x_prompt.shape[1]
    assert x_sample.shape[1] == seq and seq % RET_TB == 0 and seq % (NA_SUB * NA_TQ) == 0
    assert depth == 1, "the final norm is fused into the last layer's MLP kernel"
    rope = _rope_tables(seq)
    l = 0
    params = {
        "norm_mix_w": norm_mix_w[l].astype(F32),
        "w_in": w_in[l].astype(BF16),
        "na_bias": _na_column_table(na_rpb[l]),
        "rope": rope,
        "ret_tables": _retention_tables(ret_decay_fwd[l], ret_decay_bwd[l]),
        "w_na_out": w_na_out[l].astype(F32),
        "w_ret_out": w_ret_out[l].astype(F32),
        "w_out": w_out[l].astype(F32),
        "norm_mlp_w": norm_mlp_w[l].astype(F32),
        "w_mlp_up": w_mlp_up[l].astype(F32),
        "w_mlp_down": w_mlp_down[l].astype(F32),
        "norm_final_w": norm_final_w.astype(F32),
    }
    y_prompt, params = _encode(x_prompt, params)
    y_sample, _ = _encode(x_sample, params)
    return y_prompt, y_sample
```
